```python
import math
import jax, jax.numpy as jnp
from jax import lax
import numpy as np

D_MODEL = 1024
BATCH = 8
SEQ = 2048
DEPTH = 4
DEC_BATCH = 32
DEC_SEQ = 8
PAST_LEN = 8192
PAGE_SIZE = 128

HEAD_DIM = 64
N_HEADS = D_MODEL // HEAD_DIM
H_MOBA = N_HEADS // 2
H_SB = N_HEADS // 4
H_FOX = N_HEADS - H_MOBA - H_SB
MOBA_BLOCK = 256
MOBA_TOPK = 3
Q_BLOCK = 128
MOBA_Q_BLOCK = 32
D_IN = 4 * D_MODEL + H_FOX
EPS = 1e-6

kernel_name = "hymba_moba_stickbreak_fox_decode_step"


def rmsnorm(x, g):
    xf = x.astype(jnp.float32)
    y = xf * lax.rsqrt(jnp.mean(xf * xf, axis=-1, keepdims=True) + EPS)
    return (y * g.astype(jnp.float32)).astype(x.dtype)


def alibi_slopes(n):
    return jnp.asarray([2.0 ** (-8.0 * (i + 1) / n) for i in range(n)], dtype=jnp.float32)


def _qblock_size(tq, pref):
    return pref if tq % pref == 0 else tq


def _map_query_blocks(fn, q, pos, qb):
    b, tq = q.shape[0], q.shape[1]
    n = tq // qb
    qs = jnp.moveaxis(q.reshape((b, n, qb) + q.shape[2:]), 1, 0)
    ps = pos.reshape(n, qb)
    out = lax.map(lambda a: fn(a[0], a[1]), (qs, ps))
    return jnp.moveaxis(out, 0, 1).reshape((b, tq) + out.shape[3:])


def moba_attention(q, k, v, q_pos):
    b, t, h, dh = k.shape
    nb = -(-t // MOBA_BLOCK)
    pad = nb * MOBA_BLOCK - t
    kb = jnp.pad(k, ((0, 0), (0, pad), (0, 0), (0, 0))).reshape(b, nb, MOBA_BLOCK, h, dh).transpose(0, 3, 1, 2, 4)
    vb = jnp.pad(v, ((0, 0), (0, pad), (0, 0), (0, 0))).reshape(b, nb, MOBA_BLOCK, h, dh).transpose(0, 3, 1, 2, 4)
    k_mean = jnp.mean(kb.astype(jnp.float32), axis=3)
    slopes = alibi_slopes(h)
    n_sel = min(MOBA_TOPK, nb)
    scale = HEAD_DIM ** -0.5
    blk_ids = jnp.arange(nb)
    bi = jnp.arange(b)[:, None, None, None]
    hi = jnp.arange(h)[None, :, None, None]
    offs = jnp.arange(MOBA_BLOCK)

    def blk(q_blk, pb):
        qs = pb.shape[0]
        own = pb // MOBA_BLOCK
        gs = jnp.einsum('bqhd,bhnd->bhqn', q_blk.astype(jnp.float32), k_mean)
        gs = jnp.where(blk_ids[None, :] < own[:, None], gs, -jnp.inf)
        _, top = lax.top_k(gs, n_sel)
        own_b = jnp.broadcast_to(own[None, None, :, None], (b, h, qs, 1))
        idx = jnp.concatenate([top, own_b.astype(top.dtype)], axis=-1)
        sel_ok = jnp.arange(n_sel)[None, :] < jnp.minimum(own, MOBA_TOPK)[:, None]
        slot_ok = jnp.concatenate([sel_ok, jnp.ones((qs, 1), dtype=bool)], axis=-1)
        kg = kb[bi, hi, idx]
        vg = vb[bi, hi, idx]
        kpos = idx[..., None] * MOBA_BLOCK + offs
        qp = pb[None, None, :, None, None]
        valid = slot_ok[None, None, :, :, None] & (kpos <= qp)
        dist = (qp - kpos).astype(jnp.float32)
        s = jnp.einsum('bqhd,bhqskd->bhqsk', q_blk, kg, preferred_element_type=jnp.float32) * scale
        s = s - slopes[None, :, None, None, None] * dist
        s = jnp.where(valid, s, -jnp.inf)
        shp = s.shape
        p = jax.nn.softmax(s.reshape(shp[0], shp[1], shp[2], -1), axis=-1).reshape(shp)
        return jnp.einsum('bhqsk,bhqskd->bqhd', p.astype(vg.dtype), vg)

    return _map_query_blocks(blk, q, q_pos, _qblock_size(q.shape[1], MOBA_Q_BLOCK))


def stick_breaking_attention(q, k, v, q_pos):
    t = k.shape[1]
    k_pos = jnp.arange(t)
    scale = HEAD_DIM ** -0.5

    def blk(q_blk, pb):
        z = jnp.einsum('bqhd,bkhd->bhqk', q_blk, k, preferred_element_type=jnp.float32) * scale
        mask = k_pos[None, :] < pb[:, None]
        log_beta = jax.nn.log_sigmoid(z)
        log_1m = jnp.where(mask, jax.nn.log_sigmoid(-z), 0.0)
        after = lax.cumsum(log_1m, axis=3, reverse=True) - log_1m
        a = jnp.where(mask, jnp.exp(log_beta + after), 0.0)
        return jnp.einsum('bhqk,bkhd->bqhd', a.astype(v.dtype), v)

    return _map_query_blocks(blk, q, q_pos, _qblock_size(q.shape[1], Q_BLOCK))


def forgetting_attention(q, k, v, cumf, q_pos):
    t = k.shape[1]
    k_pos = jnp.arange(t)
    fk = jnp.moveaxis(cumf, 1, 2)
    scale = HEAD_DIM ** -0.5

    def blk(q_blk, pb):
        s = jnp.einsum('bqhd,bkhd->bhqk', q_blk, k, preferred_element_type=jnp.float32) * scale
        fq = jnp.take(fk, pb, axis=2)
        s = s + fq[..., None] - fk[:, :, None, :]
        s = jnp.where(k_pos[None, :] <= pb[:, None], s, -jnp.inf)
        p = jax.nn.softmax(s, axis=-1)
        return jnp.einsum('bhqk,bkhd->bqhd', p.astype(v.dtype), v)

    return _map_query_blocks(blk, q, q_pos, _qblock_size(q.shape[1], Q_BLOCK))


def mixer_layer(x, k_past, v_past, logf_past, norm_g, w_in, b_f, g_grp, w_out):
    b, tq, _ = x.shape
    d = D_MODEL
    h = rmsnorm(x, norm_g)
    proj = jnp.einsum('btd,de->bte', h, w_in)
    q = proj[..., 0:d].reshape(b, tq, N_HEADS, HEAD_DIM)
    k = proj[..., d:2 * d].reshape(b, tq, N_HEADS, HEAD_DIM)
    v = proj[..., 2 * d:3 * d].reshape(b, tq, N_HEADS, HEAD_DIM)
    gate = proj[..., 3 * d:4 * d]
    logf = jax.nn.log_sigmoid((proj[..., 4 * d:] + b_f).astype(jnp.float32))
    if k_past is None:
        past = 0
        k_all, v_all, logf_all = k, v, logf
    else:
        past = k_past.shape[1]
        k_all = jnp.concatenate([k_past.astype(k.dtype), k], axis=1)
        v_all = jnp.concatenate([v_past.astype(v.dtype), v], axis=1)
        logf_all = jnp.concatenate([logf_past.astype(jnp.float32), logf], axis=1)
    q_pos = past + jnp.arange(tq, dtype=jnp.int32)
    cumf = jnp.cumsum(logf_all, axis=1)
    a1, a2 = H_MOBA, H_MOBA + H_SB
    o_a = moba_attention(q[:, :, :a1], k_all[:, :, :a1], v_all[:, :, :a1], q_pos)
    o_b = stick_breaking_attention(q[:, :, a1:a2], k_all[:, :, a1:a2], v_all[:, :, a1:a2], q_pos)
    o_c = forgetting_attention(q[:, :, a2:], k_all[:, :, a2:], v_all[:, :, a2:], cumf, q_pos)
    wa, wb = a1 * HEAD_DIM, a2 * HEAD_DIM
    mix = jnp.concatenate([
        rmsnorm(o_a.reshape(b, tq, -1), g_grp[:wa]),
        rmsnorm(o_b.reshape(b, tq, -1), g_grp[wa:wb]),
        rmsnorm(o_c.reshape(b, tq, -1), g_grp[wb:])], axis=-1)
    mix = mix * jax.nn.silu(gate)
    y = x + jnp.einsum('bte,ed->btd', mix, w_out)
    return y, k, v, logf


def gather_pages(pool, page_table):
    g = pool[page_table]
    return g.reshape((g.shape[0], g.shape[1] * g.shape[2]) + g.shape[3:])


def setup_inputs(seed: int = 0) -> dict:
    key = jax.random.key(seed)
    ks = jax.random.split(key, 13)
    n_pages = PAST_LEN // PAGE_SIZE
    n_used = DEC_BATCH * n_pages
    n_phys = n_used + max(1, n_used // 4)
    x_prompt = jax.random.normal(ks[0], (BATCH, SEQ, D_MODEL), jnp.float32)
    x_sample = jax.random.normal(ks[1], (DEC_BATCH, DEC_SEQ, D_MODEL), jnp.float32)
    cache_k = jax.random.normal(ks[2], (DEPTH, n_phys, PAGE_SIZE, N_HEADS, HEAD_DIM), jnp.float32)
    cache_v = jax.random.normal(ks[3], (DEPTH, n_phys, PAGE_SIZE, N_HEADS, HEAD_DIM), jnp.float32)
    cache_logf = jax.nn.log_sigmoid(jax.random.normal(ks[4], (DEPTH, n_phys, PAGE_SIZE, H_FOX), jnp.float32))
    page_table = jax.random.permutation(ks[5], n_phys)[:n_used].reshape(DEC_BATCH, n_pages).astype(jnp.int32)
    norm_g = 1.0 + 0.02 * jax.random.normal(ks[6], (DEPTH, D_MODEL), jnp.float32)
    w_in = jax.random.normal(ks[7], (DEPTH, D_MODEL, D_IN), jnp.float32) * D_MODEL ** -0.5
    b_f = 0.1 * jax.random.normal(ks[8], (DEPTH, H_FOX), jnp.float32)
    g_grp = 1.0 + 0.02 * jax.random.normal(ks[9], (DEPTH, D_MODEL), jnp.float32)
    w_out = jax.random.normal(ks[10], (DEPTH, D_MODEL, D_MODEL), jnp.float32) * D_MODEL ** -0.5
    g_final = 1.0 + 0.02 * jax.random.normal(ks[11], (D_MODEL,), jnp.float32)
    return {"x_prompt": x_prompt, "x_sample": x_sample, "cache_k": cache_k, "cache_v": cache_v,
            "cache_logf": cache_logf, "page_table": page_table, "norm_g": norm_g, "w_in": w_in,
            "b_f": b_f, "g_grp": g_grp, "w_out": w_out, "g_final": g_final}


def reference(x_prompt, x_sample, cache_k, cache_v, cache_logf, page_table,
              norm_g, w_in, b_f, g_grp, w_out, g_final):
    hp, hs = x_prompt, x_sample
    kp, vp, fp, ksm, vsm, fsm = [], [], [], [], [], []
    for l in range(DEPTH):
        hp, k_new, v_new, f_new = mixer_layer(hp, None, None, None, norm_g[l], w_in[l], b_f[l], g_grp[l], w_out[l])
        kp.append(k_new); vp.append(v_new); fp.append(f_new)
        k_past = gather_pages(cache_k[l], page_table)
        v_past = gather_pages(cache_v[l], page_table)
        f_past = gather_pages(cache_logf[l], page_table)
        hs, k_new, v_new, f_new = mixer_layer(hs, k_past, v_past, f_past, norm_g[l], w_in[l], b_f[l], g_grp[l], w_out[l])
        ksm.append(k_new); vsm.append(v_new); fsm.append(f_new)
    y_prompt = rmsnorm(hp, g_final)
    y_sample = rmsnorm(hs, g_final)
    k_prompt = jnp.stack(kp)
    v_prompt = jnp.stack(vp)
    logf_prompt = jnp.stack(fp)
    k_sample = jnp.stack(ksm)
    v_sample = jnp.stack(vsm)
    logf_sample = jnp.stack(fsm)
    return (y_prompt, y_sample, k_prompt, v_prompt, logf_prompt, k_sample, v_sample, logf_sample)
```

```python
import functools

import jax
import jax.numpy as jnp
from jax import lax
from jax.experimental import pallas as pl
from jax.experimental.pallas import tpu as pltpu

F32 = jnp.float32
BF16 = jnp.bfloat16

D_MODEL = 1024
HEAD_DIM = 64
N_HEADS = 16
H_MOBA = 8
H_SB = 4
H_FOX = 4
W_MOBA = H_MOBA * HEAD_DIM
W_SB = H_SB * HEAD_DIM
W_FOX = H_FOX * HEAD_DIM
MOBA_BLOCK = 256
MOBA_TOPK = 3
PAGE = 128
EPS = 1e-6
SCALE = HEAD_DIM ** -0.5
NEG = -1e30
LANES = 128
BF16_ROWS = 16
TQ = MOBA_BLOCK
TM_PROJ = 512
VMEM_LIMIT = 56 * 1024 * 1024


def _nn(a, b):
    return jnp.dot(a, b, preferred_element_type=F32)


def _nt(a, b):
    return lax.dot_general(a, b, (((1,), (1,)), ((), ())), preferred_element_type=F32)


def _split2(x):
    hi = x.astype(BF16)
    lo = (x - hi.astype(F32)).astype(BF16)
    return hi, lo


def _split3(x):
    hi = x.astype(BF16)
    r = x - hi.astype(F32)
    mid = r.astype(BF16)
    lo = (r - mid.astype(F32)).astype(BF16)
    return hi, mid, lo


def _nn_exact_rhs(x, m):
    a, b, c = _split3(x)
    return _nn(a, m) + _nn(b, m) + _nn(c, m)


def _nn_exact_lhs(m, x):
    a, b, c = _split3(x)
    return _nn(m, a) + _nn(m, b) + _nn(m, c)


def _nt_exact_lhs(m, x):
    a, b, c = _split3(x)
    return _nt(m, a) + _nt(m, b) + _nt(m, c)


def _nt3(a, b):
    a_hi, a_lo = _split2(a)
    b_hi, b_lo = _split2(b)
    return _nt(a_hi, b_hi) + _nt(a_lo, b_hi) + _nt(a_hi, b_lo)


def _nn3(a, b):
    a_hi, a_lo = _split2(a)
    b_hi, b_lo = _split2(b)
    return _nn(a_hi, b_hi) + _nn(a_lo, b_hi) + _nn(a_hi, b_lo)


def _log_sigmoid(x):
    return jnp.minimum(x, 0.0) - jnp.log1p(jnp.exp(-jnp.abs(x)))


def _iota(shape, dim):
    return lax.broadcasted_iota(jnp.int32, shape, dim)


def _rms(x):
    return x * lax.rsqrt(jnp.mean(x * x, axis=-1, keepdims=True) + EPS)


def _inproj_kernel(x_ref, ng_ref, wt_ref, wft_ref, bfc_ref, bfr_ref, *rest, tiles_per_seq, prompt):
    d = D_MODEL
    h = _rms(x_ref[...]) * ng_ref[...]
    h_hi = h.astype(BF16)
    h_lo = (h - h_hi.astype(F32)).astype(BF16)
    wft_hi, wft_lo = _split2(wft_ref[...])

    def logits(nt_lhs_first):
        if nt_lhs_first:
            a, b = wft_hi[0:BF16_ROWS], wft_lo[0:BF16_ROWS]
            return _nt(a, h_hi) + _nt(a, h_lo) + _nt(b, h_hi)
        return _nt(h_hi, wft_hi) + _nt(h_lo, wft_hi) + _nt(h_hi, wft_lo)

    lf_c = _log_sigmoid(logits(False) + bfc_ref[...])

    if not prompt:
        q_ref, k_ref, v_ref, g_ref, lf_ref = rest
        q_ref[...] = _nt(h_hi, wt_ref[0:d, :])
        k_ref[...] = _nt(h_hi, wt_ref[d:2 * d, :])
        v_ref[...] = _nt(h_hi, wt_ref[2 * d:3 * d, :])
        g_ref[...] = _nt(h_hi, wt_ref[3 * d:4 * d, :])
        lf_ref[...] = lf_c
        return

    (q_ref, g_ref, kt_ref, vt_ref, ktb_ref, vtb_ref, lfr_ref, km_ref, cfc_ref, cfr_ref,
     carry_c, carry_r) = rest
    tm = x_ref.shape[0]
    nblk = tm // MOBA_BLOCK
    q_ref[...] = _nt(h_hi, wt_ref[0:d, :])
    g_ref[...] = _nt(h_hi, wt_ref[3 * d:4 * d, :])
    kt = _nt(wt_ref[d:2 * d, :], h_hi)
    kt_ref[...] = kt
    vt = _nt(wt_ref[2 * d:3 * d, :], h_hi)
    vt_ref[...] = vt
    for s in range(nblk):
        ktb_ref[s] = kt[:, s * MOBA_BLOCK:(s + 1) * MOBA_BLOCK].astype(BF16)
        vtb_ref[s] = vt[:, s * MOBA_BLOCK:(s + 1) * MOBA_BLOCK].astype(BF16)

    tok_blk = _iota((BF16_ROWS, tm), 1) >> (MOBA_BLOCK.bit_length() - 1)
    avg = jnp.where(tok_blk == _iota((BF16_ROWS, tm), 0), 1.0 / MOBA_BLOCK, 0.0).astype(BF16)
    km_ref[0] = _nt_exact_lhs(avg, kt[0:W_MOBA, :])

    @pl.when(pl.program_id(0) % tiles_per_seq == 0)
    def _():
        carry_c[...] = jnp.zeros_like(carry_c)
        carry_r[...] = jnp.zeros_like(carry_r)

    r = _iota((tm, tm), 0)
    c = _iota((tm, tm), 1)
    tri_l = jnp.where(c <= r, 1.0, 0.0).astype(BF16)
    cfc = _nn_exact_lhs(tri_l, lf_c) + carry_c[0:1, :]
    cfc_ref[...] = cfc
    carry_c[...] = jnp.broadcast_to(cfc[tm - 1:tm, :], carry_c.shape)
    lf_r = _log_sigmoid(logits(True) + bfr_ref[:, 0:1])
    lfr_ref[...] = lf_r
    tri_u = jnp.where(r <= c, 1.0, 0.0).astype(BF16)
    cfr = _nn_exact_rhs(lf_r, tri_u) + carry_r[:, 0:1]
    cfr_ref[...] = cfr
    carry_r[...] = jnp.broadcast_to(cfr[:, tm - 1:tm], carry_r.shape)


def _inproj(x, ng, wt, wft, bfc, bfr, *, batch, seq_len, prompt):
    m = x.shape[0]
    tm = min(TM_PROJ, m)
    n_tiles = m // tm
    tps = seq_len // tm if prompt else 1
    d = D_MODEL
    row = lambda i: (i, 0)
    const = lambda i: (0, 0)
    in_specs = [
        pl.BlockSpec((tm, d), row),
        pl.BlockSpec((1, d), const),
        pl.BlockSpec((4 * d, d), const),
        pl.BlockSpec((LANES, d), const),
        pl.BlockSpec((1, LANES), const),
        pl.BlockSpec((BF16_ROWS, LANES), const),
    ]
    tok_tile = pl.BlockSpec((tm, d), row)
    if prompt:
        nblk = tm // MOBA_BLOCK
        feat_tile = pl.BlockSpec((None, d, tm), lambda i: (i // tps, 0, i % tps))
        blk_tile = pl.BlockSpec((None, nblk, d, MOBA_BLOCK), lambda i: (i // tps, i % tps, 0, 0))
        out_shape = [
            jax.ShapeDtypeStruct((m, d), F32),
            jax.ShapeDtypeStruct((m, d), F32),
            jax.ShapeDtypeStruct((batch, d, seq_len), F32),
            jax.ShapeDtypeStruct((batch, d, seq_len), F32),
            jax.ShapeDtypeStruct((batch, seq_len // MOBA_BLOCK, d, MOBA_BLOCK), BF16),
            jax.ShapeDtypeStruct((batch, seq_len // MOBA_BLOCK, d, MOBA_BLOCK), BF16),
            jax.ShapeDtypeStruct((batch, BF16_ROWS, seq_len), F32),
            jax.ShapeDtypeStruct((n_tiles, BF16_ROWS, W_MOBA), F32),
            jax.ShapeDtypeStruct((m, LANES), F32),
            jax.ShapeDtypeStruct((BF16_ROWS, m), F32),
        ]
        out_specs = [
            tok_tile, tok_tile, feat_tile, feat_tile, blk_tile, blk_tile,
            pl.BlockSpec((None, BF16_ROWS, tm), lambda i: (i // tps, 0, i % tps)),
            pl.BlockSpec((1, BF16_ROWS, W_MOBA), lambda i: (i, 0, 0)),
            pl.BlockSpec((tm, LANES), row),
            pl.BlockSpec((BF16_ROWS, tm), lambda i: (0, i)),
        ]
        scratch = [pltpu.VMEM((8, LANES), F32), pltpu.VMEM((BF16_ROWS, LANES), F32)]
    else:
        out_shape = [jax.ShapeDtypeStruct((m, d), F32)] * 4 + [jax.ShapeDtypeStruct((m, LANES), F32)]
        out_specs = [tok_tile] * 4 + [pl.BlockSpec((tm, LANES), row)]
        scratch = []
    return pl.pallas_call(
        functools.partial(_inproj_kernel, tiles_per_seq=tps, prompt=prompt),
        grid=(n_tiles,),
        in_specs=in_specs,
        out_specs=out_specs,
        out_shape=out_shape,
        scratch_shapes=scratch,
        compiler_params=pltpu.CompilerParams(
            dimension_semantics=("arbitrary",), vmem_limit_bytes=VMEM_LIMIT),
        name="inproj_prompt" if prompt else "inproj_sample",
    )(x, ng, wt, wft, bfc, bfr)


def _moba_kernel(slopes_ref, q_ref, kt_ref, vt_ref, km_ref, o_ref):
    j = pl.program_id(1)
    row = _iota((TQ, TQ), 0)
    col = _iota((TQ, TQ), 1)
    rc = (row - col).astype(F32)
    causal = col <= row
    lane = _iota((TQ, LANES), 1)
    n_cand = km_ref.shape[0] - 1
    for p in range(H_MOBA // 2):
        c0 = p * LANES
        q2 = q_ref[:, c0:c0 + LANES]
        kmp = jnp.concatenate(
            [km_ref[:, c0:c0 + LANES], jnp.zeros((LANES - km_ref.shape[0], LANES), F32)], axis=0)
        kd = kt_ref[j, c0:c0 + LANES, :]
        vd = vt_ref[j, c0:c0 + LANES, :]
        consts = []
        init = []
        for hh in range(2):
            in_head = (lane >= HEAD_DIM) if hh else (lane < HEAD_DIM)
            qh = jnp.where(in_head, q2, 0.0)
            qs = (qh * SCALE).astype(BF16)
            slope = slopes_ref[2 * p + hh]
            g = _nt3(qh, kmp)
            valid = lane < j
            selm = jnp.zeros((TQ, LANES), F32)
            for i in range(n_cand):
                gi = g[:, i:i + 1]
                beats = jnp.where(valid & ((g > gi) | ((g == gi) & (lane < i))), 1.0, 0.0)
                cnt = jnp.sum(beats, axis=-1, keepdims=True)
                sel_i = jnp.where(cnt < MOBA_TOPK, 1.0, 0.0)
                selm = jnp.where(lane == i, sel_i, selm)
            s = _nn(qs, kd) - slope * rc
            s = jnp.where(causal, s, NEG)
            m = jnp.max(s, axis=-1, keepdims=True)
            pm = jnp.exp(s - m)
            l = jnp.sum(pm, axis=-1, keepdims=True)
            acc = _nt(pm.astype(BF16), vd)
            consts.append((qs, slope, selm))
            init.append((m, l, acc))

        def body(i, carry, consts=consts, c0=c0):
            kt = kt_ref[i, c0:c0 + LANES, :]
            vt = vt_ref[i, c0:c0 + LANES, :]
            dj = ((j - i) * TQ).astype(F32)
            new = []
            for hh in range(2):
                m, l, acc = carry[hh]
                qs, slope, selm = consts[hh]
                selc = jnp.sum(jnp.where(lane == i, selm, 0.0), axis=-1, keepdims=True)
                s = _nn(qs, kt) - slope * (rc + dj) + (selc - 1.0) * 1e30
                m_new = jnp.maximum(m, jnp.max(s, axis=-1, keepdims=True))
                alpha = jnp.exp(m - m_new)
                pm = jnp.exp(s - m_new)
                l = alpha * l + jnp.sum(pm, axis=-1, keepdims=True)
                acc = alpha * acc + _nt(pm.astype(BF16), vt)
                new.append((m_new, l, acc))
            return tuple(new)

        (_, l0, a0), (_, l1, a1) = lax.fori_loop(0, j, body, tuple(init))
        o_ref[:, c0:c0 + LANES] = jnp.where(lane < HEAD_DIM, a0 / l0, a1 / l1)


def _moba_prompt(slopes, q, ktb, vtb, km, *, batch, seq_len):
    m = q.shape[0]
    nq = seq_len // TQ
    kv_spec = pl.BlockSpec((None, nq, W_MOBA, TQ), lambda b, j: (b, 0, 0, 0))
    return pl.pallas_call(
        _moba_kernel,
        grid=(batch, nq),
        in_specs=[
            pl.BlockSpec(memory_space=pltpu.SMEM),
            pl.BlockSpec((TQ, W_MOBA), lambda b, j: (b * nq + j, 0)),
            kv_spec, kv_spec,
            pl.BlockSpec((None, nq, W_MOBA), lambda b, j: (b, 0, 0)),
        ],
        out_specs=pl.BlockSpec((TQ, W_MOBA), lambda b, j: (b * nq + j, 0)),
        out_shape=jax.ShapeDtypeStruct((m, W_MOBA), F32),
        compiler_params=pltpu.CompilerParams(
            dimension_semantics=("arbitrary", "arbitrary"), vmem_limit_bytes=VMEM_LIMIT),
        name="moba_prompt",
    )(slopes, q, ktb, vtb, km)


def _sb_kernel(q_ref, kt_ref, vt_ref, o_ref):
    j = pl.program_id(1)
    row = _iota((TQ, TQ), 0)
    col = _iota((TQ, TQ), 1)
    strict = col < row
    after_m = jnp.where(row > col, 1.0, 0.0).astype(BF16)
    lane = _iota((TQ, LANES), 1)

    def tile(qs, kt, vt, carry, acc, diag):
        z = _nn(qs, kt)
        lb = _log_sigmoid(z)
        l1 = lb - z
        if diag:
            l1 = jnp.where(strict, l1, 0.0)
        hi, lo = _split2(l1)
        after = _nn(hi, after_m) + _nn(lo, after_m) + carry
        a = jnp.exp(lb + after)
        if diag:
            a = jnp.where(strict, a, 0.0)
        acc = acc + _nt(a.astype(BF16), vt)
        carry = carry + jnp.sum(l1, axis=-1, keepdims=True)
        return carry, acc

    for p in range(H_SB // 2):
        c0 = p * LANES
        q2 = q_ref[:, c0:c0 + LANES]
        kd = kt_ref[j, c0:c0 + LANES, :]
        vd = vt_ref[j, c0:c0 + LANES, :]
        qss = []
        init = []
        for hh in range(2):
            in_head = (lane >= HEAD_DIM) if hh else (lane < HEAD_DIM)
            qs = (jnp.where(in_head, q2, 0.0) * SCALE).astype(BF16)
            qss.append(qs)
            init.append(tile(qs, kd, vd, jnp.zeros((TQ, 1), F32),
                             jnp.zeros((TQ, LANES), F32), True))

        def body(t, carry, qss=qss, c0=c0):
            i = j - 1 - t
            kt = kt_ref[i, c0:c0 + LANES, :]
            vt = vt_ref[i, c0:c0 + LANES, :]
            return tuple(tile(qss[hh], kt, vt, carry[hh][0], carry[hh][1], False)
                         for hh in range(2))

        (_, a0), (_, a1) = lax.fori_loop(0, j, body, tuple(init))
        o_ref[:, c0:c0 + LANES] = jnp.where(lane < HEAD_DIM, a0, a1)


def _sb_prompt(q, ktb, vtb, *, batch, seq_len):
    m = q.shape[0]
    nq = seq_len // TQ
    cb = W_MOBA // W_SB
    kv_spec = pl.BlockSpec((None, nq, W_SB, TQ), lambda b, j: (b, 0, cb, 0))
    return pl.pallas_call(
        _sb_kernel,
        grid=(batch, nq),
        in_specs=[pl.BlockSpec((TQ, W_SB), lambda b, j: (b * nq + j, cb)), kv_spec, kv_spec],
        out_specs=pl.BlockSpec((TQ, W_SB), lambda b, j: (b * nq + j, 0)),
        out_shape=jax.ShapeDtypeStruct((m, W_SB), F32),
        compiler_params=pltpu.CompilerParams(
            dimension_semantics=("arbitrary", "arbitrary"), vmem_limit_bytes=VMEM_LIMIT),
        name="sb_prompt",
    )(q, ktb, vtb)


def _fox_kernel(q_ref, kt_ref, vt_ref, cfc_ref, cfr_ref, o_ref):
    j = pl.program_id(1)
    row = _iota((TQ, TQ), 0)
    col = _iota((TQ, TQ), 1)
    causal = col <= row
    lane = _iota((TQ, LANES), 1)
    for p in range(H_FOX // 2):
        c0 = p * LANES
        q2 = q_ref[:, c0:c0 + LANES]
        kd = kt_ref[j, c0:c0 + LANES, :]
        vd = vt_ref[j, c0:c0 + LANES, :]
        consts = []
        init = []
        for hh in range(2):
            hx = 2 * p + hh
            in_head = (lane >= HEAD_DIM) if hh else (lane < HEAD_DIM)
            qs = (jnp.where(in_head, q2, 0.0) * SCALE).astype(BF16)
            fq = cfc_ref[:, hx:hx + 1]
            s = _nn(qs, kd) + fq - cfr_ref[hx, j]
            s = jnp.where(causal, s, NEG)
            m = jnp.max(s, axis=-1, keepdims=True)
            pm = jnp.exp(s - m)
            l = jnp.sum(pm, axis=-1, keepdims=True)
            acc = _nt(pm.astype(BF16), vd)
            consts.append((qs, fq, hx))
            init.append((m, l, acc))

        def body(i, carry, consts=consts, c0=c0):
            kt = kt_ref[i, c0:c0 + LANES, :]
            vt = vt_ref[i, c0:c0 + LANES, :]
            new = []
            for hh in range(2):
                m, l, acc = carry[hh]
                qs, fq, hx = consts[hh]
                s = _nn(qs, kt) + fq - cfr_ref[hx, i]
                m_new = jnp.maximum(m, jnp.max(s, axis=-1, keepdims=True))
                alpha = jnp.exp(m - m_new)
                pm = jnp.exp(s - m_new)
                l = alpha * l + jnp.sum(pm, axis=-1, keepdims=True)
                acc = alpha * acc + _nt(pm.astype(BF16), vt)
                new.append((m_new, l, acc))
            return tuple(new)

        (_, l0, a0), (_, l1, a1) = lax.fori_loop(0, j, body, tuple(init))
        o_ref[:, c0:c0 + LANES] = jnp.where(lane < HEAD_DIM, a0 / l0, a1 / l1)


def _fox_prompt(q, ktb, vtb, cfc, cfr4, *, batch, seq_len):
    m = q.shape[0]
    nq = seq_len // TQ
    cb = (W_MOBA + W_SB) // W_FOX
    kv_spec = pl.BlockSpec((None, nq, W_FOX, TQ), lambda b, j: (b, 0, cb, 0))
    return pl.pallas_call(
        _fox_kernel,
        grid=(batch, nq),
        in_specs=[
            pl.BlockSpec((TQ, W_FOX), lambda b, j: (b * nq + j, cb)),
            kv_spec, kv_spec,
            pl.BlockSpec((TQ, LANES), lambda b, j: (b * nq + j, 0)),
            pl.BlockSpec((BF16_ROWS, nq, 1, TQ), lambda b, j: (0, b, 0, 0)),
        ],
        out_specs=pl.BlockSpec((TQ, W_FOX), lambda b, j: (b * nq + j, 0)),
        out_shape=jax.ShapeDtypeStruct((m, W_FOX), F32),
        compiler_params=pltpu.CompilerParams(
            dimension_semantics=("arbitrary", "arbitrary"), vmem_limit_bytes=VMEM_LIMIT),
        name="fox_prompt",
    )(q, ktb, vtb, cfc, cfr4)


def _outproj_kernel(oa_ref, ob_ref, oc_ref, g_ref, x_ref, gg_ref, w_ref, gf_ref, y_ref, *, final):
    gg = gg_ref[...]
    a1 = W_MOBA
    a2 = W_MOBA + W_SB
    mix = jnp.concatenate([
        _rms(oa_ref[...]) * gg[:, 0:a1],
        _rms(ob_ref[...]) * gg[:, a1:a2],
        _rms(oc_ref[...]) * gg[:, a2:]], axis=-1)
    gate = g_ref[...]
    mix = mix * (gate * (1.0 / (1.0 + jnp.exp(-gate))))
    y = x_ref[...] + _nn(mix.astype(BF16), w_ref[...])
    if final:
        y = _rms(y) * gf_ref[...]
    y_ref[...] = y


def _outproj(oa, ob, oc, gate, x, gg, w, gf, *, final):
    m = x.shape[0]
    tm = min(TM_PROJ, m)
    d = D_MODEL
    row = lambda i: (i, 0)
    const = lambda i: (0, 0)
    return pl.pallas_call(
        functools.partial(_outproj_kernel, final=final),
        grid=(m // tm,),
        in_specs=[
            pl.BlockSpec((tm, W_MOBA), row),
            pl.BlockSpec((tm, W_SB), row),
            pl.BlockSpec((tm, W_FOX), row),
            pl.BlockSpec((tm, d), row),
            pl.BlockSpec((tm, d), row),
            pl.BlockSpec((1, d), const),
            pl.BlockSpec((d, d), const),
            pl.BlockSpec((1, d), const),
        ],
        out_specs=pl.BlockSpec((tm, d), row),
        out_shape=jax.ShapeDtypeStruct((m, d), F32),
        compiler_params=pltpu.CompilerParams(
            dimension_semantics=("arbitrary",), vmem_limit_bytes=VMEM_LIMIT),
        name="outproj",
    )(oa, ob, oc, gate, x, gg, w, gf)


R_ALL = 128
R_MOBA = 64
R_SB = 32
R_FOX = 32


def _sample_attn_kernel(pt_ref, q_ref, kn_ref, vn_ref, lfn_ref, slope_ref,
                        ka_ref, kb_ref, va_ref, vb_ref, lfa_ref, lfb_ref,
                        oa_ref, ob_ref, oc_ref,
                        qbd_sc, qm_sc, km_sc, mm_sc, lm_sc, accm_sc,
                        cs_sc, accs_sc, mf_sc, lf_sc, accf_sc, cf_sc, cn_sc,
                        *, n_past_blocks, dec_seq, past_len):
    del pt_ref
    step = pl.program_id(1)
    nb = n_past_blocks
    blk = MOBA_BLOCK
    c_m0, c_s0, c_f0 = 0, W_MOBA, W_MOBA + W_SB
    r_s0, r_f0 = R_MOBA, R_MOBA + R_SB
    lane_m = _iota((R_MOBA, LANES), 1)
    t_m = _iota((R_MOBA, 1), 0) & (dec_seq - 1)
    slope_m = slope_ref[:, 0:1]

    @pl.when(step == 0)
    def _new_rows():
        q = q_ref[...]
        head_of_lane = _iota((dec_seq, D_MODEL), 1) >> 6
        qbd = jnp.concatenate(
            [jnp.where(head_of_lane == h, q, 0.0) for h in range(N_HEADS)], axis=0)
        qbd_b = (qbd * SCALE).astype(BF16)
        qbd_sc[...] = qbd_b
        qm_sc[...] = qbd[0:R_MOBA, 0:W_MOBA]
        km_sc[...] = jnp.zeros_like(km_sc)
        pad = jnp.zeros((LANES - dec_seq, D_MODEL), F32)
        knp = jnp.concatenate([kn_ref[...], pad], axis=0).astype(BF16)
        vnp = jnp.concatenate([vn_ref[...], pad], axis=0).astype(BF16)
        s_all = _nt(qbd_b, knp)
        c = _iota((1, LANES), 1)

        s = s_all[0:R_MOBA] - slope_m * (t_m - c).astype(F32)
        s = jnp.where(c <= t_m, s, NEG)
        m = jnp.max(s, axis=-1, keepdims=True)
        pm = jnp.exp(s - m)
        l = jnp.sum(pm, axis=-1, keepdims=True)
        mm_sc[...] = jnp.where(lane_m == nb, m, NEG)
        lm_sc[...] = jnp.where(lane_m == nb, l, 0.0)
        accm_sc[nb] = _nn(pm.astype(BF16), vnp[:, c_m0:c_m0 + W_MOBA])

        t_s = _iota((R_SB, 1), 0) & (dec_seq - 1)
        z = s_all[r_s0:r_s0 + R_SB]
        lb = _log_sigmoid(z)
        mask_s = c < t_s
        l1 = jnp.where(mask_s, lb - z, 0.0)
        rr = _iota((LANES, LANES), 0)
        cc = _iota((LANES, LANES), 1)
        after_m = jnp.where(rr > cc, 1.0, 0.0).astype(BF16)
        hi, lo = _split2(l1)
        after = _nn(hi, after_m) + _nn(lo, after_m)
        a = jnp.where(mask_s, jnp.exp(lb + after), 0.0)
        accs_sc[...] = _nn(a.astype(BF16), vnp[:, c_s0:c_s0 + W_SB])
        cs_sc[...] = jnp.broadcast_to(jnp.sum(l1, axis=-1, keepdims=True), cs_sc.shape)

        t_f = t_s
        lfpad = jnp.concatenate(
            [lfn_ref[...], jnp.zeros((LANES - dec_seq, LANES), F32)], axis=0)
        head_of_row = _iota((R_FOX, LANES), 0) >> 3
        expand = jnp.where(_iota((R_FOX, LANES), 1) == head_of_row, 1.0, 0.0).astype(BF16)
        lfrow = _nt_exact_lhs(expand, lfpad)
        incl_m = jnp.where(rr <= cc, 1.0, 0.0).astype(BF16)
        cnrow = _nn_exact_rhs(lfrow, incl_m)
        cncol = jnp.sum(jnp.where(c == t_f, cnrow, 0.0), axis=-1, keepdims=True)
        s = s_all[r_f0:r_f0 + R_FOX] + cncol - cnrow
        s = jnp.where(c <= t_f, s, NEG)
        m = jnp.max(s, axis=-1, keepdims=True)
        pm = jnp.exp(s - m)
        mf_sc[...] = jnp.broadcast_to(m, mf_sc.shape)
        lf_sc[...] = jnp.broadcast_to(jnp.sum(pm, axis=-1, keepdims=True), lf_sc.shape)
        accf_sc[...] = _nn(pm.astype(BF16), vnp[:, c_f0:c_f0 + W_FOX])
        cn_sc[...] = jnp.broadcast_to(cncol, cn_sc.shape)
        cf_sc[...] = jnp.zeros_like(cf_sc)

    @pl.when(step > 0)
    def _past_block():
        pb = nb - step
        kf = jnp.concatenate([ka_ref[...], kb_ref[...]], axis=1)
        kt = kf.astype(BF16)
        vt = jnp.concatenate([va_ref[...], vb_ref[...]], axis=1).astype(BF16)
        kmean = jnp.mean(kf[c_m0:c_m0 + W_MOBA, :], axis=1, keepdims=True)
        km_sc[...] = jnp.where(_iota(km_sc.shape, 1) == pb, kmean, km_sc[...])
        qbd = qbd_sc[...]
        c = _iota((1, blk), 1)
        rr = _iota((blk, blk), 0)
        cc = _iota((blk, blk), 1)
        after_m = jnp.where(rr > cc, 1.0, 0.0).astype(BF16)

        s = _nn(qbd[0:R_MOBA, c_m0:c_m0 + W_MOBA], kt[c_m0:c_m0 + W_MOBA, :])
        dist = (past_len + t_m - (pb * blk + c)).astype(F32)
        s = s - slope_m * dist
        m = jnp.max(s, axis=-1, keepdims=True)
        pm = jnp.exp(s - m)
        l = jnp.sum(pm, axis=-1, keepdims=True)
        mm_sc[...] = jnp.where(lane_m == pb, m, mm_sc[...])
        lm_sc[...] = jnp.where(lane_m == pb, l, lm_sc[...])
        accm_sc[pb] = _nt(pm.astype(BF16), vt[c_m0:c_m0 + W_MOBA, :])

        z = _nn(qbd[r_s0:r_s0 + R_SB, c_s0:c_s0 + W_SB], kt[c_s0:c_s0 + W_SB, :])
        lb = _log_sigmoid(z)
        l1 = lb - z
        hi, lo = _split2(l1)
        after = _nn(hi, after_m) + _nn(lo, after_m) + cs_sc[:, 0:1]
        a = jnp.exp(lb + after)
        accs_sc[...] = accs_sc[...] + _nt(a.astype(BF16), vt[c_s0:c_s0 + W_SB, :])
        cs_sc[...] = cs_sc[...] + jnp.sum(l1, axis=-1, keepdims=True)

        lf = jnp.concatenate([lfa_ref[...], lfb_ref[...]], axis=1)
        lfe = jnp.concatenate(
            [jnp.broadcast_to(lf[h:h + 1, :], (dec_seq, blk)) for h in range(H_FOX)], axis=0)
        suffix = _nn_exact_rhs(lfe, after_m) + cf_sc[:, 0:1]
        s = _nn(qbd[r_f0:r_f0 + R_FOX, c_f0:c_f0 + W_FOX], kt[c_f0:c_f0 + W_FOX, :])
        s = s + cn_sc[:, 0:1] + suffix
        m_old = mf_sc[:, 0:1]
        m_new = jnp.maximum(m_old, jnp.max(s, axis=-1, keepdims=True))
        alpha = jnp.exp(m_old - m_new)
        pm = jnp.exp(s - m_new)
        l_new = alpha * lf_sc[:, 0:1] + jnp.sum(pm, axis=-1, keepdims=True)
        accf_sc[...] = alpha * accf_sc[...] + _nt(pm.astype(BF16), vt[c_f0:c_f0 + W_FOX, :])
        mf_sc[...] = jnp.broadcast_to(m_new, mf_sc.shape)
        lf_sc[...] = jnp.broadcast_to(l_new, lf_sc.shape)
        cf_sc[...] = cf_sc[...] + jnp.sum(lfe, axis=-1, keepdims=True)

    @pl.when(step == nb)
    def _finish():
        g = _nn3(qm_sc[...], km_sc[...])
        valid = lane_m < nb
        selm = jnp.where(lane_m == nb, 1.0, 0.0)
        for i in range(nb):
            gi = g[:, i:i + 1]
            beats = jnp.where(valid & ((g > gi) | ((g == gi) & (lane_m < i))), 1.0, 0.0)
            cnt = jnp.sum(beats, axis=-1, keepdims=True)
            sel_i = jnp.where(cnt < MOBA_TOPK, 1.0, 0.0)
            selm = jnp.where(lane_m == i, sel_i, selm)
        taken = selm > 0.5
        mall = mm_sc[...]
        mfin = jnp.max(jnp.where(taken, mall, NEG), axis=-1, keepdims=True)
        w = jnp.where(taken, jnp.exp(jnp.minimum(mall - mfin, 0.0)), 0.0)
        lfin = jnp.sum(w * lm_sc[...], axis=-1, keepdims=True)
        acc = jnp.zeros((R_MOBA, W_MOBA), F32)
        for b in range(nb + 1):
            acc = acc + w[:, b:b + 1] * accm_sc[b]
        om = acc / lfin

        def diag_blocks(o, n_heads):
            width = n_heads * HEAD_DIM
            head_of_lane = _iota((dec_seq, width), 1) >> 6
            out = jnp.zeros((dec_seq, width), F32)
            for h in range(n_heads):
                out = out + jnp.where(head_of_lane == h, o[h * dec_seq:(h + 1) * dec_seq, :], 0.0)
            return out

        oa_ref[...] = diag_blocks(om, H_MOBA)
        ob_ref[...] = diag_blocks(accs_sc[...], H_SB)
        oc_ref[...] = diag_blocks(accf_sc[...] / lf_sc[:, 0:1], H_FOX)


def _sample_attn(pt, q, kn, vn, lfn, slope_rows, ckt, cvt, clf, *, layer, dec_batch, dec_seq, n_pages):
    m = q.shape[0]
    d = D_MODEL
    pages_per_block = MOBA_BLOCK // PAGE
    nb = n_pages // pages_per_block
    past_len = n_pages * PAGE

    def page_map(off):
        def index_map(b, s, pt_ref):
            pb = nb - jnp.maximum(s, 1)
            return (layer, pt_ref[b * n_pages + pages_per_block * pb + off], 0, 0)
        return index_map

    tok = lambda b, s, pt_ref: (b, 0)
    const = lambda b, s, pt_ref: (0, 0)
    page_kv = lambda off: pl.BlockSpec((None, None, d, PAGE), page_map(off))
    page_lf = lambda off: pl.BlockSpec((None, None, H_FOX, PAGE), page_map(off))
    grid_spec = pltpu.PrefetchScalarGridSpec(
        num_scalar_prefetch=1,
        grid=(dec_batch, nb + 1),
        in_specs=[
            pl.BlockSpec((dec_seq, d), tok),
            pl.BlockSpec((dec_seq, d), tok),
            pl.BlockSpec((dec_seq, d), tok),
            pl.BlockSpec((dec_seq, LANES), tok),
            pl.BlockSpec((R_MOBA, LANES), const),
            page_kv(0), page_kv(1), page_kv(0), page_kv(1),
            page_lf(0), page_lf(1),
        ],
        out_specs=[
            pl.BlockSpec((dec_seq, W_MOBA), tok),
            pl.BlockSpec((dec_seq, W_SB), tok),
            pl.BlockSpec((dec_seq, W_FOX), tok),
        ],
        scratch_shapes=[
            pltpu.VMEM((R_ALL, d), BF16),
            pltpu.VMEM((R_MOBA, W_MOBA), F32),
            pltpu.VMEM((W_MOBA, LANES), F32),
            pltpu.VMEM((R_MOBA, LANES), F32),
            pltpu.VMEM((R_MOBA, LANES), F32),
            pltpu.VMEM((nb + 1, R_MOBA, W_MOBA), F32),
            pltpu.VMEM((R_SB, LANES), F32),
            pltpu.VMEM((R_SB, W_SB), F32),
            pltpu.VMEM((R_FOX, LANES), F32),
            pltpu.VMEM((R_FOX, LANES), F32),
            pltpu.VMEM((R_FOX, W_FOX), F32),
            pltpu.VMEM((R_FOX, LANES), F32),
            pltpu.VMEM((R_FOX, LANES), F32),
        ],
    )
    return pl.pallas_call(
        functools.partial(_sample_attn_kernel, n_past_blocks=nb, dec_seq=dec_seq, past_len=past_len),
        grid_spec=grid_spec,
        out_shape=[jax.ShapeDtypeStruct((m, W_MOBA), F32),
                   jax.ShapeDtypeStruct((m, W_SB), F32),
                   jax.ShapeDtypeStruct((m, W_FOX), F32)],
        compiler_params=pltpu.CompilerParams(
            dimension_semantics=("arbitrary", "arbitrary"), vmem_limit_bytes=VMEM_LIMIT),
        name="sample_attn",
    )(pt, q, kn, vn, lfn, slope_rows, ckt, ckt, cvt, cvt, clf, clf)


def kernel(x_prompt, x_sample, cache_k, cache_v, cache_logf, page_table,
           norm_g, w_in, b_f, g_grp, w_out, g_final):
    batch, seq_len, d = x_prompt.shape
    dec_batch, dec_seq, _ = x_sample.shape
    depth, n_phys = cache_k.shape[0], cache_k.shape[1]
    n_pages = page_table.shape[1]
    assert d == D_MODEL and seq_len % TM_PROJ == 0 and cache_k.shape[2] == PAGE
    assert dec_seq == 8 and (n_pages * PAGE) % MOBA_BLOCK == 0 and n_pages * PAGE // MOBA_BLOCK < LANES

    hp = x_prompt.reshape(batch * seq_len, d)
    hs = x_sample.reshape(dec_batch * dec_seq, d)
    ckt = jnp.transpose(cache_k, (0, 1, 3, 4, 2)).reshape(depth, n_phys, d, PAGE)
    cvt = jnp.transpose(cache_v, (0, 1, 3, 4, 2)).reshape(depth, n_phys, d, PAGE)
    clf = jnp.swapaxes(cache_logf, 2, 3)
    pt = page_table.reshape(-1).astype(jnp.int32)
    slopes = jnp.asarray([2.0 ** (-8.0 * (i + 1) / H_MOBA) for i in range(H_MOBA)], dtype=F32)
    slope_rows = jnp.broadcast_to(jnp.repeat(slopes, dec_seq)[:, None], (R_MOBA, LANES))
    gf = g_final.reshape(1, d)
    w_in_t = jnp.transpose(w_in, (2, 0, 1))

    kp, vp, fp, ksm, vsm, fsm = [], [], [], [], [], []
    nblk = seq_len // MOBA_BLOCK
    for l in range(depth):
        wt = w_in_t[:4 * d, l, :].astype(BF16)
        wft = jnp.pad(w_in_t[4 * d:, l, :], ((0, LANES - H_FOX), (0, 0)))
        bfc = jnp.pad(b_f[l].reshape(1, H_FOX), ((0, 0), (0, LANES - H_FOX)))
        bfr = jnp.broadcast_to(jnp.pad(b_f[l], (0, BF16_ROWS - H_FOX))[:, None], (BF16_ROWS, LANES))
        ng = norm_g[l].reshape(1, d)
        gg = g_grp[l].reshape(1, d)
        wo = w_out[l].astype(BF16)
        final = l == depth - 1

        q, gate, kt, vt, ktb, vtb, lfr, km, cfc, cfr = _inproj(
            hp, ng, wt, wft, bfc, bfr, batch=batch, seq_len=seq_len, prompt=True)
        km = km[:, :TM_PROJ // MOBA_BLOCK, :].reshape(batch, nblk, W_MOBA)
        cfr4 = cfr.reshape(BF16_ROWS, batch * nblk, 1, TQ)
        oa = _moba_prompt(slopes, q, ktb, vtb, km, batch=batch, seq_len=seq_len)
        ob = _sb_prompt(q, ktb, vtb, batch=batch, seq_len=seq_len)
        oc = _fox_prompt(q, ktb, vtb, cfc, cfr4, batch=batch, seq_len=seq_len)
        hp = _outproj(oa, ob, oc, gate, hp, gg, wo, gf, final=final)
        kp.append(kt)
        vp.append(vt)
        fp.append(lfr[:, :H_FOX, :])

        q, k, v, gate, lf = _inproj(
            hs, ng, wt, wft, bfc, bfr, batch=dec_batch, seq_len=dec_seq, prompt=False)
        oa, ob, oc = _sample_attn(pt, q, k, v, lf, slope_rows, ckt, cvt, clf, layer=l,
                                  dec_batch=dec_batch, dec_seq=dec_seq, n_pages=n_pages)
        hs = _outproj(oa, ob, oc, gate, hs, gg, wo, gf, final=final)
        ksm.append(k)
        vsm.append(v)
        fsm.append(lf[:, :H_FOX])

    hd = (N_HEADS, HEAD_DIM)

    def token_major(xs):
        x = jnp.stack(xs).reshape((depth, batch) + hd + (seq_len,))
        return jnp.transpose(x, (0, 1, 4, 2, 3))

    return (hp.reshape(batch, seq_len, d),
            hs.reshape(dec_batch, dec_seq, d),
            token_major(kp),
            token_major(vp),
            jnp.swapaxes(jnp.stack(fp), 2, 3),
            jnp.stack(ksm).reshape((depth, dec_batch, dec_seq) + hd),
            jnp.stack(vsm).reshape((depth, dec_batch, dec_seq) + hd),
            jnp.stack(fsm).reshape(depth, dec_batch, dec_seq, H_FOX))
```

```python
import functools

import jax
import jax.numpy as jnp
from jax import lax
from jax.experimental import pallas as pl
from jax.experimental.pallas import tpu as pltpu

F32 = jnp.float32
BF16 = jnp.bfloat16

D_MODEL = 1024
HEAD_DIM = 64
N_HEADS = 16
H_MOBA = 8
H_SB = 4
H_FOX = 4
W_MOBA = H_MOBA * HEAD_DIM
W_SB = H_SB * HEAD_DIM
W_FOX = H_FOX * HEAD_DIM
MOBA_BLOCK = 256
MOBA_TOPK = 3
PAGE = 128
EPS = 1e-6
SCALE = HEAD_DIM ** -0.5
NEG = -1e30
SKIP_LOG = -110.0
LANES = 128
BF16_ROWS = 16
TQ = MOBA_BLOCK
TM_PROJ = 512
CHUNK_BLOCKS = 4
VMEM_LIMIT = 56 * 1024 * 1024


def _nn(a, b):
    return jnp.dot(a, b, preferred_element_type=F32)


def _nt(a, b):
    return lax.dot_general(a, b, (((1,), (1,)), ((), ())), preferred_element_type=F32)


def _split2(x):
    hi = x.astype(BF16)
    lo = (x - hi.astype(F32)).astype(BF16)
    return hi, lo


def _split3(x):
    hi = x.astype(BF16)
    r = x - hi.astype(F32)
    mid = r.astype(BF16)
    lo = (r - mid.astype(F32)).astype(BF16)
    return hi, mid, lo


def _nn_exact_rhs(x, m):
    a, b, c = _split3(x)
    return _nn(a, m) + _nn(b, m) + _nn(c, m)


def _nn_exact_lhs(m, x):
    a, b, c = _split3(x)
    return _nn(m, a) + _nn(m, b) + _nn(m, c)


def _nt_exact_lhs(m, x):
    a, b, c = _split3(x)
    return _nt(m, a) + _nt(m, b) + _nt(m, c)


def _nn3(a, b):
    a_hi, a_lo = _split2(a)
    b_hi, b_lo = _split2(b)
    return _nn(a_hi, b_hi) + _nn(a_lo, b_hi) + _nn(a_hi, b_lo)


def _log_sigmoid(x):
    return jnp.minimum(x, 0.0) - jnp.log1p(jnp.exp(-jnp.abs(x)))


def _iota(shape, dim):
    return lax.broadcasted_iota(jnp.int32, shape, dim)


def _rms(x):
    return x * lax.rsqrt(jnp.mean(x * x, axis=-1, keepdims=True) + EPS)


def _inproj_kernel(x_ref, ng_ref, wt_ref, wft_ref, bfc_ref, bfr_ref, *rest, tiles_per_seq, prompt):
    d = D_MODEL
    h = _rms(x_ref[...]) * ng_ref[...]
    h_hi = h.astype(BF16)
    h_lo = (h - h_hi.astype(F32)).astype(BF16)
    wft_hi, wft_lo = _split2(wft_ref[...])
    lg_c = _nt(h_hi, wft_hi) + _nt(h_lo, wft_hi) + _nt(h_hi, wft_lo)
    lf_c = _log_sigmoid(lg_c + bfc_ref[...])

    if not prompt:
        q_ref, k_ref, v_ref, g_ref, lf_ref = rest
        q_ref[...] = _nt(h_hi, wt_ref[0:d, :])
        k_ref[...] = _nt(h_hi, wt_ref[d:2 * d, :])
        v_ref[...] = _nt(h_hi, wt_ref[2 * d:3 * d, :])
        g_ref[...] = _nt(h_hi, wt_ref[3 * d:4 * d, :])
        lf_ref[...] = lf_c
        return

    (_, _, _, q_ref, g_ref, kt_ref, vt_ref, lfr_ref, ktok_ref, vtb_ref, km_ref, cfr_ref,
     cfrep_ref, carry_c, carry_r) = rest
    tm = x_ref.shape[0]
    nblk = tm // MOBA_BLOCK
    q_ref[...] = _nt(h_hi, wt_ref[0:d, :])
    g_ref[...] = _nt(h_hi, wt_ref[3 * d:4 * d, :])
    ktok_ref[...] = _nt(h_hi, wt_ref[d:2 * d, :]).astype(BF16)
    kt = _nt(wt_ref[d:2 * d, :], h_hi)
    kt_ref[...] = kt
    vt = _nt(wt_ref[2 * d:3 * d, :], h_hi)
    vt_ref[...] = vt
    for s in range(nblk):
        vtb_ref[s] = vt[:, s * MOBA_BLOCK:(s + 1) * MOBA_BLOCK].astype(BF16)

    tok_blk = _iota((BF16_ROWS, tm), 1) >> (MOBA_BLOCK.bit_length() - 1)
    avg = jnp.where(tok_blk == _iota((BF16_ROWS, tm), 0), 1.0 / MOBA_BLOCK, 0.0).astype(BF16)
    km_ref[0] = _nt_exact_lhs(avg, kt[0:W_MOBA, :])

    @pl.when(pl.program_id(0) % tiles_per_seq == 0)
    def _():
        carry_c[...] = jnp.zeros_like(carry_c)
        carry_r[...] = jnp.zeros_like(carry_r)

    r = _iota((tm, tm), 0)
    c = _iota((tm, tm), 1)
    tri_l = jnp.where(c <= r, 1.0, 0.0).astype(BF16)
    cfc = _nn_exact_lhs(tri_l, lf_c) + carry_c[0:1, :]
    carry_c[...] = jnp.broadcast_to(cfc[tm - 1:tm, :], carry_c.shape)
    for hx in range(H_FOX):
        cfrep_ref[hx] = jnp.broadcast_to(cfc[:, hx:hx + 1], (tm, LANES))
    a, b = wft_hi[0:BF16_ROWS], wft_lo[0:BF16_ROWS]
    lg_r = _nt(a, h_hi) + _nt(a, h_lo) + _nt(b, h_hi)
    lf_r = _log_sigmoid(lg_r + bfr_ref[:, 0:1])
    lfr_ref[...] = lf_r[0:H_FOX]
    tri_u = jnp.where(r <= c, 1.0, 0.0).astype(BF16)
    cfr = _nn_exact_rhs(lf_r, tri_u) + carry_r[:, 0:1]
    cfr_ref[...] = cfr
    carry_r[...] = jnp.broadcast_to(cfr[:, tm - 1:tm], carry_r.shape)


def _inproj(x, ng, wt, wft, bfc, bfr, layer_bufs=None, *, layer=0, batch, seq_len, prompt):
    m = x.shape[0]
    tm = min(TM_PROJ, m)
    n_tiles = m // tm
    tps = seq_len // tm if prompt else 1
    d = D_MODEL
    row = lambda i: (i, 0)
    const = lambda i: (0, 0)
    in_specs = [
        pl.BlockSpec((tm, d), row),
        pl.BlockSpec((1, d), const),
        pl.BlockSpec((4 * d, d), const, pipeline_mode=pl.Buffered(1)),
        pl.BlockSpec((LANES, d), const),
        pl.BlockSpec((1, LANES), const),
        pl.BlockSpec((BF16_ROWS, LANES), const),
    ]
    tok_tile = pl.BlockSpec((tm, d), row)
    args = [x, ng, wt, wft, bfc, bfr]
    aliases = {}
    if prompt:
        nblk = tm // MOBA_BLOCK
        n_seq_blk = seq_len // MOBA_BLOCK
        kt_all, vt_all, lf_all = layer_bufs
        in_specs += [pl.BlockSpec(memory_space=pl.ANY)] * 3
        args += [kt_all, vt_all, lf_all]
        aliases = {6: 2, 7: 3, 8: 4}
        feat_tile = pl.BlockSpec((None, None, d, tm), lambda i: (layer, i // tps, 0, i % tps))
        out_shape = [
            jax.ShapeDtypeStruct((m, d), F32),
            jax.ShapeDtypeStruct((m, d), F32),
            jax.ShapeDtypeStruct(kt_all.shape, F32),
            jax.ShapeDtypeStruct(vt_all.shape, F32),
            jax.ShapeDtypeStruct(lf_all.shape, F32),
            jax.ShapeDtypeStruct((m, d), BF16),
            jax.ShapeDtypeStruct((batch, n_seq_blk, d, MOBA_BLOCK), BF16),
            jax.ShapeDtypeStruct((n_tiles, BF16_ROWS, W_MOBA), F32),
            jax.ShapeDtypeStruct((BF16_ROWS, m), F32),
            jax.ShapeDtypeStruct((H_FOX, m, LANES), F32),
        ]
        out_specs = [
            tok_tile, tok_tile, feat_tile, feat_tile,
            pl.BlockSpec((None, None, H_FOX, tm), lambda i: (layer, i // tps, 0, i % tps)),
            tok_tile,
            pl.BlockSpec((None, nblk, d, MOBA_BLOCK), lambda i: (i // tps, i % tps, 0, 0)),
            pl.BlockSpec((1, BF16_ROWS, W_MOBA), lambda i: (i, 0, 0)),
            pl.BlockSpec((BF16_ROWS, tm), lambda i: (0, i)),
            pl.BlockSpec((H_FOX, tm, LANES), lambda i: (0, i, 0)),
        ]
        scratch = [pltpu.VMEM((8, LANES), F32), pltpu.VMEM((BF16_ROWS, LANES), F32)]
    else:
        out_shape = [jax.ShapeDtypeStruct((m, d), F32)] * 4 + [jax.ShapeDtypeStruct((m, LANES), F32)]
        out_specs = [tok_tile] * 4 + [pl.BlockSpec((tm, LANES), row)]
        scratch = []
    return pl.pallas_call(
        functools.partial(_inproj_kernel, tiles_per_seq=tps, prompt=prompt),
        grid=(n_tiles,),
        in_specs=in_specs,
        out_specs=out_specs,
        out_shape=out_shape,
        scratch_shapes=scratch,
        input_output_aliases=aliases,
        compiler_params=pltpu.CompilerParams(
            dimension_semantics=("arbitrary",), vmem_limit_bytes=VMEM_LIMIT),
        name="inproj_prompt" if prompt else "inproj_sample",
    )(*args)


def _tile_consts():
    krow = _iota((TQ, TQ), 0)
    qcol = _iota((TQ, TQ), 1)
    return krow, qcol, _iota((LANES, TQ), 0)


def _moba_kernel(slopes_ref, q_ref, k_ref, vt_ref, km_ref, o_ref):
    j = pl.program_id(1)
    krow, qcol, frow = _tile_consts()
    qk_off = (qcol - krow).astype(F32)
    causal = krow <= qcol
    brow = _iota((BF16_ROWS, TQ), 0)
    n_blk = km_ref.shape[0]
    r0 = pl.multiple_of(j * TQ, TQ)
    for p in range(H_MOBA // 2):
        c0 = p * LANES
        q_t = q_ref[:, c0:c0 + LANES].T
        kmp = jnp.concatenate(
            [km_ref[:, c0:c0 + LANES], jnp.zeros((BF16_ROWS - n_blk, LANES), F32)], axis=0)
        kd = k_ref[pl.ds(r0, TQ), c0:c0 + LANES]
        consts = []
        init = []
        for hh in range(2):
            f0 = c0 + hh * HEAD_DIM
            in_head = (frow >= HEAD_DIM) if hh else (frow < HEAD_DIM)
            qh = jnp.where(in_head, q_t, 0.0)
            qs = (qh * SCALE).astype(BF16)
            slope = slopes_ref[2 * p + hh]
            g = _nn3(kmp, qh)
            valid = brow < j
            sel = jnp.zeros((BF16_ROWS, TQ), F32)
            for i in range(n_blk - 1):
                gi = g[i:i + 1, :]
                beats = jnp.where(valid & ((g > gi) | ((g == gi) & (brow < i))), 1.0, 0.0)
                cnt = jnp.sum(beats, axis=0, keepdims=True)
                sel = jnp.where(brow == i, jnp.where(cnt < MOBA_TOPK, 1.0, 0.0), sel)
            s = _nn(kd, qs) - slope * qk_off
            s = jnp.where(causal, s, NEG)
            m = jnp.max(s, axis=0, keepdims=True)
            pm = jnp.exp(s - m)
            l = jnp.sum(pm, axis=0, keepdims=True)
            acc = _nn(vt_ref[j, f0:f0 + HEAD_DIM, :], pm.astype(BF16))
            consts.append((qs, slope, sel, f0))
            init.append((m, l, acc))

        def body(i, carry, consts=consts, c0=c0):
            kt = k_ref[pl.ds(pl.multiple_of(i * TQ, TQ), TQ), c0:c0 + LANES]
            dj = ((j - i) * TQ).astype(F32)
            new = []
            for hh in range(2):
                m, l, acc = carry[hh]
                qs, slope, sel, f0 = consts[hh]
                taken = jnp.sum(jnp.where(brow == i, sel, 0.0), axis=0, keepdims=True)
                bias = (taken - 1.0) * 1e30 - slope * dj
                s = _nn(kt, qs) - slope * qk_off + bias
                m_new = jnp.maximum(m, jnp.max(s, axis=0, keepdims=True))
                alpha = jnp.exp(m - m_new)
                pm = jnp.exp(s - m_new)
                l = alpha * l + jnp.sum(pm, axis=0, keepdims=True)
                acc = alpha * acc + _nn(vt_ref[i, f0:f0 + HEAD_DIM, :], pm.astype(BF16))
                new.append((m_new, l, acc))
            return tuple(new)

        (_, l0, a0), (_, l1, a1) = lax.fori_loop(0, j, body, tuple(init))
        o_ref[:, c0:c0 + LANES] = jnp.concatenate([a0 / l0, a1 / l1], axis=0).T


def _sb_kernel(q_ref, k_ref, vt_ref, o_ref):
    j = pl.program_id(1)
    krow, qcol, frow = _tile_consts()
    strict = krow < qcol
    later = jnp.where(qcol > krow, 1.0, 0.0).astype(BF16)
    r0 = pl.multiple_of(j * TQ, TQ)

    def tile(qs, kt, vt, carry, acc, diag):
        z = _nn(kt, qs)
        lb = _log_sigmoid(z)
        l1 = lb - z
        if diag:
            l1 = jnp.where(strict, l1, 0.0)
        hi, lo = _split2(l1)
        after = _nn(later, hi) + _nn(later, lo) + carry
        a = jnp.exp(lb + after)
        if diag:
            a = jnp.where(strict, a, 0.0)
        acc = acc + _nn(vt, a.astype(BF16))
        carry = carry + jnp.sum(l1, axis=0, keepdims=True)
        return carry, acc

    for p in range(H_SB // 2):
        c0 = p * LANES
        q_t = q_ref[:, c0:c0 + LANES].T
        kd = k_ref[pl.ds(r0, TQ), c0:c0 + LANES]
        qss = []
        init = []
        for hh in range(2):
            f0 = c0 + hh * HEAD_DIM
            in_head = (frow >= HEAD_DIM) if hh else (frow < HEAD_DIM)
            qs = (jnp.where(in_head, q_t, 0.0) * SCALE).astype(BF16)
            qss.append(qs)
            init.append(tile(qs, kd, vt_ref[j, f0:f0 + HEAD_DIM, :], jnp.zeros((1, TQ), F32),
                             jnp.zeros((HEAD_DIM, TQ), F32), True))

        def worst(state):
            return jnp.maximum(jnp.max(state[0][0]), jnp.max(state[1][0]))

        def cond(st):
            t, top, _ = st
            return jnp.logical_and(t < j, top > SKIP_LOG)

        def body(st, qss=qss, c0=c0):
            t, _, state = st
            i = j - 1 - t
            kt = k_ref[pl.ds(pl.multiple_of(i * TQ, TQ), TQ), c0:c0 + LANES]
            new = tuple(
                tile(qss[hh], kt, vt_ref[i, c0 + hh * HEAD_DIM:c0 + (hh + 1) * HEAD_DIM, :],
                     state[hh][0], state[hh][1], False) for hh in range(2))
            return t + 1, worst(new), new

        init = tuple(init)
        _, _, ((_, a0), (_, a1)) = lax.while_loop(cond, body, (jnp.int32(0), worst(init), init))
        o_ref[:, c0:c0 + LANES] = jnp.concatenate([a0, a1], axis=0).T


def _fox_kernel(q_ref, k_ref, vt_ref, cfr_ref, cfrep_ref, o_ref, kmax_sc):
    j = pl.program_id(1)
    krow, qcol, frow = _tile_consts()
    causal = krow <= qcol
    r0 = pl.multiple_of(j * TQ, TQ)

    @pl.when(j == 0)
    def _():
        lane = _iota((1, LANES), 1)
        for p in range(H_FOX // 2):
            col_max = jnp.max(jnp.abs(k_ref[:, p * LANES:(p + 1) * LANES].astype(F32)),
                              axis=0, keepdims=True)
            for hh in range(2):
                in_head = (lane >= HEAD_DIM) if hh else (lane < HEAD_DIM)
                top = jnp.max(jnp.where(in_head, col_max, 0.0), axis=1, keepdims=True)
                hx = 2 * p + hh
                kmax_sc[hx:hx + 1, :] = jnp.broadcast_to(top, (1, LANES))

    for p in range(H_FOX // 2):
        c0 = p * LANES
        q_t = q_ref[:, c0:c0 + LANES].T
        kd = k_ref[pl.ds(r0, TQ), c0:c0 + LANES]
        consts = []
        init = []
        for hh in range(2):
            hx = 2 * p + hh
            f0 = c0 + hh * HEAD_DIM
            in_head = (frow >= HEAD_DIM) if hh else (frow < HEAD_DIM)
            qs = (jnp.where(in_head, q_t, 0.0) * SCALE).astype(BF16)
            fq = cfr_ref[hx, j]
            fk = cfrep_ref[hx, pl.ds(r0, TQ), :]
            s = _nn(kd, qs) + fq - jnp.concatenate([fk, fk], axis=1)
            s = jnp.where(causal, s, NEG)
            m = jnp.max(s, axis=0, keepdims=True)
            pm = jnp.exp(s - m)
            l = jnp.sum(pm, axis=0, keepdims=True)
            acc = _nn(vt_ref[j, f0:f0 + HEAD_DIM, :], pm.astype(BF16))
            cap = jnp.sum(jnp.abs(qs.astype(F32)), axis=0, keepdims=True) * kmax_sc[hx:hx + 1, 0:1] + fq
            consts.append((qs, fq, cap, hx, f0))
            init.append((m, l, acc))

        def headroom(i, state, consts=consts):
            tops = []
            for hh in range(2):
                _, _, cap, hx, _ = consts[hh]
                f_end = cfr_ref[hx, jnp.maximum(i, 0)][:, TQ - 1:TQ]
                tops.append(jnp.max(cap - f_end - state[hh][0]))
            return jnp.maximum(tops[0], tops[1])

        def cond(st):
            t, top, _ = st
            return jnp.logical_and(t < j, top > SKIP_LOG)

        def body(st, consts=consts, c0=c0):
            t, _, state = st
            i = j - 1 - t
            ri = pl.multiple_of(i * TQ, TQ)
            kt = k_ref[pl.ds(ri, TQ), c0:c0 + LANES]
            new = []
            for hh in range(2):
                m, l, acc = state[hh]
                qs, fq, _, hx, f0 = consts[hh]
                fk = cfrep_ref[hx, pl.ds(ri, TQ), :]
                s = _nn(kt, qs) + fq - jnp.concatenate([fk, fk], axis=1)
                m_new = jnp.maximum(m, jnp.max(s, axis=0, keepdims=True))
                alpha = jnp.exp(m - m_new)
                pm = jnp.exp(s - m_new)
                l = alpha * l + jnp.sum(pm, axis=0, keepdims=True)
                acc = alpha * acc + _nn(vt_ref[i, f0:f0 + HEAD_DIM, :], pm.astype(BF16))
                new.append((m_new, l, acc))
            new = tuple(new)
            return t + 1, headroom(i - 1, new), new

        init = tuple(init)
        _, _, ((_, l0, a0), (_, l1, a1)) = lax.while_loop(
            cond, body, (jnp.int32(0), headroom(j - 1, init), init))
        o_ref[:, c0:c0 + LANES] = jnp.concatenate([a0 / l0, a1 / l1], axis=0).T


def _prompt_attn(kind, q, ktok, vtb, extra, *, batch, seq_len):
    m = q.shape[0]
    nq = seq_len // TQ
    width, cb = {"moba": (W_MOBA, 0), "sb": (W_SB, W_MOBA // W_SB),
                 "fox": (W_FOX, (W_MOBA + W_SB) // W_FOX)}[kind]
    q_spec = pl.BlockSpec((TQ, width), lambda b, j: (b * nq + j, cb))
    k_spec = pl.BlockSpec((seq_len, width), lambda b, j: (b, cb))
    v_spec = pl.BlockSpec((None, nq, width, TQ), lambda b, j: (b, 0, cb, 0))
    in_specs = [q_spec, k_spec, v_spec]
    args = [q, ktok, vtb]
    scratch = []
    if kind == "moba":
        slopes, km = extra
        in_specs = [pl.BlockSpec(memory_space=pltpu.SMEM)] + in_specs
        in_specs.append(pl.BlockSpec((None, nq, W_MOBA), lambda b, j: (b, 0, 0)))
        args = [slopes] + args + [km]
        body = _moba_kernel
    elif kind == "sb":
        body = _sb_kernel
    else:
        cfr4, cfrep = extra
        in_specs.append(pl.BlockSpec((BF16_ROWS, nq, 1, TQ), lambda b, j: (0, b, 0, 0)))
        in_specs.append(pl.BlockSpec((H_FOX, seq_len, LANES), lambda b, j: (0, b, 0)))
        args += [cfr4, cfrep]
        scratch = [pltpu.VMEM((8, LANES), F32)]
        body = _fox_kernel
    return pl.pallas_call(
        body,
        grid=(batch, nq),
        in_specs=in_specs,
        out_specs=pl.BlockSpec((TQ, width), lambda b, j: (b * nq + j, 0)),
        out_shape=jax.ShapeDtypeStruct((m, width), F32),
        scratch_shapes=scratch,
        compiler_params=pltpu.CompilerParams(
            dimension_semantics=("arbitrary", "arbitrary"), vmem_limit_bytes=VMEM_LIMIT),
        name=kind + "_prompt",
    )(*args)


def _outproj_kernel(oa_ref, ob_ref, oc_ref, g_ref, x_ref, gg_ref, w_ref, gf_ref, y_ref, *, final):
    gg = gg_ref[...]
    a1 = W_MOBA
    a2 = W_MOBA + W_SB
    mix = jnp.concatenate([
        _rms(oa_ref[...]) * gg[:, 0:a1],
        _rms(ob_ref[...]) * gg[:, a1:a2],
        _rms(oc_ref[...]) * gg[:, a2:]], axis=-1)
    gate = g_ref[...]
    mix = mix * (gate * (1.0 / (1.0 + jnp.exp(-gate))))
    y = x_ref[...] + _nn(mix.astype(BF16), w_ref[...])
    if final:
        y = _rms(y) * gf_ref[...]
    y_ref[...] = y


def _outproj(oa, ob, oc, gate, x, gg, w, gf, *, final):
    m = x.shape[0]
    tm = min(TM_PROJ, m)
    d = D_MODEL
    row = lambda i: (i, 0)
    const = lambda i: (0, 0)
    return pl.pallas_call(
        functools.partial(_outproj_kernel, final=final),
        grid=(m // tm,),
        in_specs=[
            pl.BlockSpec((tm, W_MOBA), row),
            pl.BlockSpec((tm, W_SB), row),
            pl.BlockSpec((tm, W_FOX), row),
            pl.BlockSpec((tm, d), row),
            pl.BlockSpec((tm, d), row),
            pl.BlockSpec((1, d), const),
            pl.BlockSpec((d, d), const),
            pl.BlockSpec((1, d), const),
        ],
        out_specs=pl.BlockSpec((tm, d), row),
        out_shape=jax.ShapeDtypeStruct((m, d), F32),
        compiler_params=pltpu.CompilerParams(
            dimension_semantics=("arbitrary",), vmem_limit_bytes=VMEM_LIMIT),
        name="outproj",
    )(oa, ob, oc, gate, x, gg, w, gf)


R_ALL = 128
R_MOBA = 64
R_SB = 32
R_FOX = 32
PAGES_PER_BLOCK = MOBA_BLOCK // PAGE
CHUNK_PAGES = CHUNK_BLOCKS * PAGES_PER_BLOCK


def _sample_attn_kernel(pt_ref, q_ref, kn_ref, vn_ref, lfn_ref, slope_ref, *rest,
                        n_chunks, dec_seq, past_len):
    del pt_ref
    k_refs = rest[0:CHUNK_PAGES]
    v_refs = rest[CHUNK_PAGES:2 * CHUNK_PAGES]
    lf_refs = rest[2 * CHUNK_PAGES:3 * CHUNK_PAGES]
    oa_ref, ob_ref, oc_ref = rest[3 * CHUNK_PAGES:3 * CHUNK_PAGES + 3]
    (qbd_sc, qm_sc, km_sc, mm_sc, lm_sc, accm_sc,
     cs_sc, accs_sc, mf_sc, lf_sc, accf_sc, cf_sc, cn_sc) = rest[3 * CHUNK_PAGES + 3:]
    step = pl.program_id(1)
    nb = n_chunks * CHUNK_BLOCKS
    blk = MOBA_BLOCK
    c_m0, c_s0, c_f0 = 0, W_MOBA, W_MOBA + W_SB
    r_s0, r_f0 = R_MOBA, R_MOBA + R_SB
    lane_m = _iota((R_MOBA, LANES), 1)
    t_m = _iota((R_MOBA, 1), 0) & (dec_seq - 1)
    slope_m = slope_ref[:, 0:1]

    @pl.when(step == 0)
    def _new_rows():
        q = q_ref[...]
        head_of_lane = _iota((dec_seq, D_MODEL), 1) >> 6
        qbd = jnp.concatenate(
            [jnp.where(head_of_lane == h, q, 0.0) for h in range(N_HEADS)], axis=0)
        qbd_b = (qbd * SCALE).astype(BF16)
        qbd_sc[...] = qbd_b
        qm_sc[...] = qbd[0:R_MOBA, 0:W_MOBA]
        km_sc[...] = jnp.zeros_like(km_sc)
        pad = jnp.zeros((LANES - dec_seq, D_MODEL), F32)
        knp = jnp.concatenate([kn_ref[...], pad], axis=0).astype(BF16)
        vnp = jnp.concatenate([vn_ref[...], pad], axis=0).astype(BF16)
        s_all = _nt(qbd_b, knp)
        c = _iota((1, LANES), 1)

        s = s_all[0:R_MOBA] - slope_m * (t_m - c).astype(F32)
        s = jnp.where(c <= t_m, s, NEG)
        m = jnp.max(s, axis=-1, keepdims=True)
        pm = jnp.exp(s - m)
        l = jnp.sum(pm, axis=-1, keepdims=True)
        mm_sc[...] = jnp.where(lane_m == nb, m, NEG)
        lm_sc[...] = jnp.where(lane_m == nb, l, 0.0)
        accm_sc[nb] = _nn(pm.astype(BF16), vnp[:, c_m0:c_m0 + W_MOBA])

        t_s = _iota((R_SB, 1), 0) & (dec_seq - 1)
        z = s_all[r_s0:r_s0 + R_SB]
        lb = _log_sigmoid(z)
        mask_s = c < t_s
        l1 = jnp.where(mask_s, lb - z, 0.0)
        rr = _iota((LANES, LANES), 0)
        cc = _iota((LANES, LANES), 1)
        after_m = jnp.where(rr > cc, 1.0, 0.0).astype(BF16)
        hi, lo = _split2(l1)
        after = _nn(hi, after_m) + _nn(lo, after_m)
        a = jnp.where(mask_s, jnp.exp(lb + after), 0.0)
        accs_sc[...] = _nn(a.astype(BF16), vnp[:, c_s0:c_s0 + W_SB])
        cs_sc[...] = jnp.broadcast_to(jnp.sum(l1, axis=-1, keepdims=True), cs_sc.shape)

        t_f = t_s
        lfpad = jnp.concatenate(
            [lfn_ref[...], jnp.zeros((LANES - dec_seq, LANES), F32)], axis=0)
        head_of_row = _iota((R_FOX, LANES), 0) >> 3
        expand = jnp.where(_iota((R_FOX, LANES), 1) == head_of_row, 1.0, 0.0).astype(BF16)
        lfrow = _nt_exact_lhs(expand, lfpad)
        incl_m = jnp.where(rr <= cc, 1.0, 0.0).astype(BF16)
        cnrow = _nn_exact_rhs(lfrow, incl_m)
        cncol = jnp.sum(jnp.where(c == t_f, cnrow, 0.0), axis=-1, keepdims=True)
        s = s_all[r_f0:r_f0 + R_FOX] + cncol - cnrow
        s = jnp.where(c <= t_f, s, NEG)
        m = jnp.max(s, axis=-1, keepdims=True)
        pm = jnp.exp(s - m)
        mf_sc[...] = jnp.broadcast_to(m, mf_sc.shape)
        lf_sc[...] = jnp.broadcast_to(jnp.sum(pm, axis=-1, keepdims=True), lf_sc.shape)
        accf_sc[...] = _nn(pm.astype(BF16), vnp[:, c_f0:c_f0 + W_FOX])
        cn_sc[...] = jnp.broadcast_to(cncol, cn_sc.shape)
        cf_sc[...] = jnp.zeros_like(cf_sc)

    @pl.when(step > 0)
    def _past_chunk():
        chunk = n_chunks - step
        qbd = qbd_sc[...]
        c = _iota((1, blk), 1)
        rr = _iota((blk, blk), 0)
        cc = _iota((blk, blk), 1)
        after_m = jnp.where(rr > cc, 1.0, 0.0).astype(BF16)
        lane_km = _iota(km_sc.shape, 1)

        def block_rows(refs, b, r0, width):
            return jnp.concatenate(
                [refs[PAGES_PER_BLOCK * b + o][r0:r0 + width, :] for o in range(PAGES_PER_BLOCK)],
                axis=1)

        km = km_sc[...]
        mm = mm_sc[...]
        lm = lm_sc[...]
        q_m = qbd[0:R_MOBA, c_m0:c_m0 + W_MOBA]
        for b in range(CHUNK_BLOCKS):
            pb = chunk * CHUNK_BLOCKS + b
            kf = block_rows(k_refs, b, c_m0, W_MOBA)
            km = jnp.where(lane_km == pb, jnp.mean(kf, axis=1, keepdims=True), km)
            s = _nn(q_m, kf.astype(BF16))
            dist = (past_len + t_m - (pb * blk + c)).astype(F32)
            s = s - slope_m * dist
            m = jnp.max(s, axis=-1, keepdims=True)
            pm = jnp.exp(s - m)
            l = jnp.sum(pm, axis=-1, keepdims=True)
            mm = jnp.where(lane_m == pb, m, mm)
            lm = jnp.where(lane_m == pb, l, lm)
            accm_sc[pb] = _nt(pm.astype(BF16), block_rows(v_refs, b, c_m0, W_MOBA).astype(BF16))
        km_sc[...] = km
        mm_sc[...] = mm
        lm_sc[...] = lm

        q_s = qbd[r_s0:r_s0 + R_SB, c_s0:c_s0 + W_SB]
        carry = cs_sc[:, 0:1]
        acc = accs_sc[...]
        for b in reversed(range(CHUNK_BLOCKS)):
            z = _nn(q_s, block_rows(k_refs, b, c_s0, W_SB).astype(BF16))
            lb = _log_sigmoid(z)
            l1 = lb - z
            hi, lo = _split2(l1)
            after = _nn(hi, after_m) + _nn(lo, after_m) + carry
            a = jnp.exp(lb + after)
            acc = acc + _nt(a.astype(BF16), block_rows(v_refs, b, c_s0, W_SB).astype(BF16))
            carry = carry + jnp.sum(l1, axis=-1, keepdims=True)
        accs_sc[...] = acc
        cs_sc[...] = jnp.broadcast_to(carry, cs_sc.shape)

        q_f = qbd[r_f0:r_f0 + R_FOX, c_f0:c_f0 + W_FOX]
        cn = cn_sc[:, 0:1]
        cf = cf_sc[:, 0:1]
        logits = []
        for b in reversed(range(CHUNK_BLOCKS)):
            lf = jnp.concatenate(
                [lf_refs[PAGES_PER_BLOCK * b + o][...] for o in range(PAGES_PER_BLOCK)], axis=1)
            lfe = jnp.concatenate(
                [jnp.broadcast_to(lf[h:h + 1, :], (dec_seq, blk)) for h in range(H_FOX)], axis=0)
            suffix = _nn_exact_rhs(lfe, after_m) + cf
            s = _nn(q_f, block_rows(k_refs, b, c_f0, W_FOX).astype(BF16)) + cn + suffix
            logits.append((b, s))
            cf = cf + jnp.sum(lfe, axis=-1, keepdims=True)
        m_old = mf_sc[:, 0:1]
        m_new = m_old
        for _, s in logits:
            m_new = jnp.maximum(m_new, jnp.max(s, axis=-1, keepdims=True))
        alpha = jnp.exp(m_old - m_new)
        l_new = alpha * lf_sc[:, 0:1]
        acc = alpha * accf_sc[...]
        for b, s in logits:
            pm = jnp.exp(s - m_new)
            l_new = l_new + jnp.sum(pm, axis=-1, keepdims=True)
            acc = acc + _nt(pm.astype(BF16), block_rows(v_refs, b, c_f0, W_FOX).astype(BF16))
        accf_sc[...] = acc
        mf_sc[...] = jnp.broadcast_to(m_new, mf_sc.shape)
        lf_sc[...] = jnp.broadcast_to(l_new, lf_sc.shape)
        cf_sc[...] = jnp.broadcast_to(cf, cf_sc.shape)

    @pl.when(step == n_chunks)
    def _finish():
        g = _nn3(qm_sc[...], km_sc[...])
        valid = lane_m < nb
        selm = jnp.where(lane_m == nb, 1.0, 0.0)
        for i in range(nb):
            gi = g[:, i:i + 1]
            beats = jnp.where(valid & ((g > gi) | ((g == gi) & (lane_m < i))), 1.0, 0.0)
            cnt = jnp.sum(beats, axis=-1, keepdims=True)
            sel_i = jnp.where(cnt < MOBA_TOPK, 1.0, 0.0)
            selm = jnp.where(lane_m == i, sel_i, selm)
        taken = selm > 0.5
        mall = mm_sc[...]
        mfin = jnp.max(jnp.where(taken, mall, NEG), axis=-1, keepdims=True)
        w = jnp.where(taken, jnp.exp(jnp.minimum(mall - mfin, 0.0)), 0.0)
        lfin = jnp.sum(w * lm_sc[...], axis=-1, keepdims=True)
        acc = jnp.zeros((R_MOBA, W_MOBA), F32)
        for b in range(nb + 1):
            acc = acc + w[:, b:b + 1] * accm_sc[b]
        om = acc / lfin

        def diag_blocks(o, n_heads):
            width = n_heads * HEAD_DIM
            head_of_lane = _iota((dec_seq, width), 1) >> 6
            out = jnp.zeros((dec_seq, width), F32)
            for h in range(n_heads):
                out = out + jnp.where(head_of_lane == h, o[h * dec_seq:(h + 1) * dec_seq, :], 0.0)
            return out

        oa_ref[...] = diag_blocks(om, H_MOBA)
        ob_ref[...] = diag_blocks(accs_sc[...], H_SB)
        oc_ref[...] = diag_blocks(accf_sc[...] / lf_sc[:, 0:1], H_FOX)


def _sample_attn(pt, q, kn, vn, lfn, slope_rows, ckt, cvt, clf, *, layer, dec_batch, dec_seq, n_pages):
    m = q.shape[0]
    d = D_MODEL
    n_chunks = n_pages // CHUNK_PAGES
    nb = n_chunks * CHUNK_BLOCKS
    past_len = n_pages * PAGE

    def page_map(off):
        def index_map(b, s, pt_ref):
            chunk = n_chunks - jnp.maximum(s, 1)
            return (layer, pt_ref[b * n_pages + CHUNK_PAGES * chunk + off], 0, 0)
        return index_map

    tok = lambda b, s, pt_ref: (b, 0)
    const = lambda b, s, pt_ref: (0, 0)
    page_kv = [pl.BlockSpec((None, None, d, PAGE), page_map(o)) for o in range(CHUNK_PAGES)]
    page_lf = [pl.BlockSpec((None, None, H_FOX, PAGE), page_map(o)) for o in range(CHUNK_PAGES)]
    grid_spec = pltpu.PrefetchScalarGridSpec(
        num_scalar_prefetch=1,
        grid=(dec_batch, n_chunks + 1),
        in_specs=[
            pl.BlockSpec((dec_seq, d), tok),
            pl.BlockSpec((dec_seq, d), tok),
            pl.BlockSpec((dec_seq, d), tok),
            pl.BlockSpec((dec_seq, LANES), tok),
            pl.BlockSpec((R_MOBA, LANES), const),
        ] + page_kv + page_kv + page_lf,
        out_specs=[
            pl.BlockSpec((dec_seq, W_MOBA), tok),
            pl.BlockSpec((dec_seq, W_SB), tok),
            pl.BlockSpec((dec_seq, W_FOX), tok),
        ],
        scratch_shapes=[
            pltpu.VMEM((R_ALL, d), BF16),
            pltpu.VMEM((R_MOBA, W_MOBA), F32),
            pltpu.VMEM((W_MOBA, LANES), F32),
            pltpu.VMEM((R_MOBA, LANES), F32),
            pltpu.VMEM((R_MOBA, LANES), F32),
            pltpu.VMEM((nb + 1, R_MOBA, W_MOBA), F32),
            pltpu.VMEM((R_SB, LANES), F32),
            pltpu.VMEM((R_SB, W_SB), F32),
            pltpu.VMEM((R_FOX, LANES), F32),
            pltpu.VMEM((R_FOX, LANES), F32),
            pltpu.VMEM((R_FOX, W_FOX), F32),
            pltpu.VMEM((R_FOX, LANES), F32),
            pltpu.VMEM((R_FOX, LANES), F32),
        ],
    )
    return pl.pallas_call(
        functools.partial(_sample_attn_kernel, n_chunks=n_chunks, dec_seq=dec_seq, past_len=past_len),
        grid_spec=grid_spec,
        out_shape=[jax.ShapeDtypeStruct((m, W_MOBA), F32),
                   jax.ShapeDtypeStruct((m, W_SB), F32),
                   jax.ShapeDtypeStruct((m, W_FOX), F32)],
        compiler_params=pltpu.CompilerParams(
            dimension_semantics=("arbitrary", "arbitrary"), vmem_limit_bytes=VMEM_LIMIT),
        name="sample_attn",
    )(pt, q, kn, vn, lfn, slope_rows, *([ckt] * CHUNK_PAGES), *([cvt] * CHUNK_PAGES),
      *([clf] * CHUNK_PAGES))


def kernel(x_prompt, x_sample, cache_k, cache_v, cache_logf, page_table,
           norm_g, w_in, b_f, g_grp, w_out, g_final):
    batch, seq_len, d = x_prompt.shape
    dec_batch, dec_seq, _ = x_sample.shape
    depth, n_phys = cache_k.shape[0], cache_k.shape[1]
    n_pages = page_table.shape[1]
    assert d == D_MODEL and seq_len % TM_PROJ == 0 and cache_k.shape[2] == PAGE
    assert seq_len // MOBA_BLOCK <= BF16_ROWS
    assert dec_seq == 8 and n_pages % CHUNK_PAGES == 0 and n_pages // PAGES_PER_BLOCK < LANES

    hp = x_prompt.reshape(batch * seq_len, d)
    hs = x_sample.reshape(dec_batch * dec_seq, d)
    ckt = jnp.transpose(cache_k, (0, 1, 3, 4, 2)).reshape(depth, n_phys, d, PAGE)
    cvt = jnp.transpose(cache_v, (0, 1, 3, 4, 2)).reshape(depth, n_phys, d, PAGE)
    clf = jnp.swapaxes(cache_logf, 2, 3)
    pt = page_table.reshape(-1).astype(jnp.int32)
    slopes = jnp.asarray([2.0 ** (-8.0 * (i + 1) / H_MOBA) for i in range(H_MOBA)], dtype=F32)
    slope_rows = jnp.broadcast_to(jnp.repeat(slopes, dec_seq)[:, None], (R_MOBA, LANES))
    gf = g_final.reshape(1, d)
    w_in_t = jnp.transpose(w_in, (2, 0, 1))

    bufs = (jnp.zeros((depth, batch, d, seq_len), F32),
            jnp.zeros((depth, batch, d, seq_len), F32),
            jnp.zeros((depth, batch, H_FOX, seq_len), F32))
    ksm, vsm, fsm = [], [], []
    nblk = seq_len // MOBA_BLOCK
    for l in range(depth):
        wt = w_in_t[:4 * d, l, :].astype(BF16)
        wft = jnp.pad(w_in_t[4 * d:, l, :], ((0, LANES - H_FOX), (0, 0)))
        bfc = jnp.pad(b_f[l].reshape(1, H_FOX), ((0, 0), (0, LANES - H_FOX)))
        bfr = jnp.broadcast_to(jnp.pad(b_f[l], (0, BF16_ROWS - H_FOX))[:, None], (BF16_ROWS, LANES))
        ng = norm_g[l].reshape(1, d)
        gg = g_grp[l].reshape(1, d)
        wo = w_out[l].astype(BF16)
        final = l == depth - 1

        q, gate, kt_all, vt_all, lf_all, ktok, vtb, km, cfr, cfrep = _inproj(
            hp, ng, wt, wft, bfc, bfr, bufs, layer=l, batch=batch, seq_len=seq_len, prompt=True)
        bufs = (kt_all, vt_all, lf_all)
        km = km[:, :TM_PROJ // MOBA_BLOCK, :].reshape(batch, nblk, W_MOBA)
        cfr4 = cfr.reshape(BF16_ROWS, batch * nblk, 1, TQ)
        oa = _prompt_attn("moba", q, ktok, vtb, (slopes, km), batch=batch, seq_len=seq_len)
        ob = _prompt_attn("sb", q, ktok, vtb, None, batch=batch, seq_len=seq_len)
        oc = _prompt_attn("fox", q, ktok, vtb, (cfr4, cfrep), batch=batch, seq_len=seq_len)
        hp = _outproj(oa, ob, oc, gate, hp, gg, wo, gf, final=final)

        q, k, v, gate, lf = _inproj(
            hs, ng, wt, wft, bfc, bfr, batch=dec_batch, seq_len=dec_seq, prompt=False)
        oa, ob, oc = _sample_attn(pt, q, k, v, lf, slope_rows, ckt, cvt, clf, layer=l,
                                  dec_batch=dec_batch, dec_seq=dec_seq, n_pages=n_pages)
        hs = _outproj(oa, ob, oc, gate, hs, gg, wo, gf, final=final)
        ksm.append(k)
        vsm.append(v)
        fsm.append(lf[:, :H_FOX])

    hd = (N_HEADS, HEAD_DIM)

    def token_major(x):
        return jnp.transpose(x.reshape((depth, batch) + hd + (seq_len,)), (0, 1, 4, 2, 3))

    kt_all, vt_all, lf_all = bufs
    return (hp.reshape(batch, seq_len, d),
            hs.reshape(dec_batch, dec_seq, d),
            token_major(kt_all),
            token_major(vt_all),
            jnp.swapaxes(lf_all, 2, 3),
            jnp.stack(ksm).reshape((depth, dec_batch, dec_seq) + hd),
            jnp.stack(vsm).reshape((depth, dec_batch, dec_seq) + hd),
            jnp.stack(fsm).reshape(depth, dec_batch, dec_seq, H_FOX))
```

```python
import functools

import jax
import jax.numpy as jnp
from jax import lax
from jax.experimental import pallas as pl
from jax.experimental.pallas import tpu as pltpu

F32 = jnp.float32
BF16 = jnp.bfloat16

D_MODEL = 1024
HEAD_DIM = 64
N_HEADS = 16
H_MOBA = 8
H_SB = 4
H_FOX = 4
W_MOBA = H_MOBA * HEAD_DIM
W_SB = H_SB * HEAD_DIM
W_FOX = H_FOX * HEAD_DIM
MOBA_BLOCK = 256
MOBA_TOPK = 3
PAGE = 128
EPS = 1e-6
SCALE = HEAD_DIM ** -0.5
NEG = -1e30
SKIP_LOG = -110.0
LANES = 128
BF16_ROWS = 16
TQ = MOBA_BLOCK
TM_PROJ = 512
CHUNK_BLOCKS = 4
VMEM_LIMIT = 56 * 1024 * 1024


def _nn(a, b):
    return jnp.dot(a, b, preferred_element_type=F32)


def _nt(a, b):
    return lax.dot_general(a, b, (((1,), (1,)), ((), ())), preferred_element_type=F32)


def _split2(x):
    hi = x.astype(BF16)
    lo = (x - hi.astype(F32)).astype(BF16)
    return hi, lo


def _split3(x):
    hi = x.astype(BF16)
    r = x - hi.astype(F32)
    mid = r.astype(BF16)
    lo = (r - mid.astype(F32)).astype(BF16)
    return hi, mid, lo


def _nn_exact_rhs(x, m):
    a, b, c = _split3(x)
    return _nn(a, m) + _nn(b, m) + _nn(c, m)


def _nn_exact_lhs(m, x):
    a, b, c = _split3(x)
    return _nn(m, a) + _nn(m, b) + _nn(m, c)


def _nt_exact_lhs(m, x):
    a, b, c = _split3(x)
    return _nt(m, a) + _nt(m, b) + _nt(m, c)


def _nn3(a, b):
    a_hi, a_lo = _split2(a)
    b_hi, b_lo = _split2(b)
    return _nn(a_hi, b_hi) + _nn(a_lo, b_hi) + _nn(a_hi, b_lo)


def _log_sigmoid(x):
    return jnp.minimum(x, 0.0) - jnp.log1p(jnp.exp(-jnp.abs(x)))


def _iota(shape, dim):
    return lax.broadcasted_iota(jnp.int32, shape, dim)


def _rms(x):
    return x * lax.rsqrt(jnp.mean(x * x, axis=-1, keepdims=True) + EPS)


def _inproj_kernel(x_ref, ng_ref, wt_ref, wft_ref, bfc_ref, bfr_ref, *rest, tiles_per_seq, prompt):
    d = D_MODEL
    h = _rms(x_ref[...]) * ng_ref[...]
    h_hi = h.astype(BF16)
    h_lo = (h - h_hi.astype(F32)).astype(BF16)
    wft_hi, wft_lo = _split2(wft_ref[...])
    lg_c = _nt(h_hi, wft_hi) + _nt(h_lo, wft_hi) + _nt(h_hi, wft_lo)
    lf_c = _log_sigmoid(lg_c + bfc_ref[...])

    if not prompt:
        q_ref, k_ref, v_ref, g_ref, lf_ref = rest
        q_ref[...] = _nt(h_hi, wt_ref[0:d, :])
        k_ref[...] = _nt(h_hi, wt_ref[d:2 * d, :])
        v_ref[...] = _nt(h_hi, wt_ref[2 * d:3 * d, :])
        g_ref[...] = _nt(h_hi, wt_ref[3 * d:4 * d, :])
        lf_ref[...] = lf_c
        return

    (_, _, _, q_ref, g_ref, kt_ref, vt_ref, lfr_ref, ktok_ref, vtb_ref, km_ref, cfr_ref,
     cfrep_ref, carry_c, carry_r) = rest
    tm = x_ref.shape[0]
    nblk = tm // MOBA_BLOCK
    q_ref[...] = _nt(h_hi, wt_ref[0:d, :])
    g_ref[...] = _nt(h_hi, wt_ref[3 * d:4 * d, :])
    kt = _nt(wt_ref[d:2 * d, :], h_hi)
    kt_ref[...] = kt
    ktok_ref[...] = kt.T.astype(BF16)
    vt = _nt(wt_ref[2 * d:3 * d, :], h_hi)
    vt_ref[...] = vt
    for s in range(nblk):
        vtb_ref[s] = vt[:, s * MOBA_BLOCK:(s + 1) * MOBA_BLOCK].astype(BF16)

    tok_blk = _iota((BF16_ROWS, tm), 1) >> (MOBA_BLOCK.bit_length() - 1)
    avg = jnp.where(tok_blk == _iota((BF16_ROWS, tm), 0), 1.0 / MOBA_BLOCK, 0.0).astype(BF16)
    km_ref[0] = _nt_exact_lhs(avg, kt[0:W_MOBA, :])

    @pl.when(pl.program_id(0) % tiles_per_seq == 0)
    def _():
        carry_c[...] = jnp.zeros_like(carry_c)
        carry_r[...] = jnp.zeros_like(carry_r)

    r = _iota((tm, tm), 0)
    c = _iota((tm, tm), 1)
    tri_l = jnp.where(c <= r, 1.0, 0.0).astype(BF16)
    cfc = _nn_exact_lhs(tri_l, lf_c) + carry_c[0:1, :]
    carry_c[...] = jnp.broadcast_to(cfc[tm - 1:tm, :], carry_c.shape)
    for hx in range(H_FOX):
        cfrep_ref[hx] = jnp.broadcast_to(cfc[:, hx:hx + 1], (tm, LANES))
    a, b = wft_hi[0:BF16_ROWS], wft_lo[0:BF16_ROWS]
    lg_r = _nt(a, h_hi) + _nt(a, h_lo) + _nt(b, h_hi)
    lf_r = _log_sigmoid(lg_r + bfr_ref[:, 0:1])
    lfr_ref[...] = lf_r[0:H_FOX]
    tri_u = jnp.where(r <= c, 1.0, 0.0).astype(BF16)
    cfr = _nn_exact_rhs(lf_r, tri_u) + carry_r[:, 0:1]
    cfr_ref[...] = cfr
    carry_r[...] = jnp.broadcast_to(cfr[:, tm - 1:tm], carry_r.shape)


def _inproj(x, ng, wt, wft, bfc, bfr, layer_bufs=None, *, layer=0, batch, seq_len, prompt):
    m = x.shape[0]
    tm = min(TM_PROJ, m)
    n_tiles = m // tm
    tps = seq_len // tm if prompt else 1
    d = D_MODEL
    row = lambda i: (i, 0)
    const = lambda i: (0, 0)
    in_specs = [
        pl.BlockSpec((tm, d), row),
        pl.BlockSpec((1, d), const),
        pl.BlockSpec((4 * d, d), const, pipeline_mode=pl.Buffered(1)),
        pl.BlockSpec((LANES, d), const),
        pl.BlockSpec((1, LANES), const),
        pl.BlockSpec((BF16_ROWS, LANES), const),
    ]
    tok_tile = pl.BlockSpec((tm, d), row)
    args = [x, ng, wt, wft, bfc, bfr]
    aliases = {}
    if prompt:
        nblk = tm // MOBA_BLOCK
        n_seq_blk = seq_len // MOBA_BLOCK
        kt_all, vt_all, lf_all = layer_bufs
        in_specs += [pl.BlockSpec(memory_space=pl.ANY)] * 3
        args += [kt_all, vt_all, lf_all]
        aliases = {6: 2, 7: 3, 8: 4}
        feat_tile = pl.BlockSpec((None, None, d, tm), lambda i: (layer, i // tps, 0, i % tps))
        out_shape = [
            jax.ShapeDtypeStruct((m, d), F32),
            jax.ShapeDtypeStruct((m, d), F32),
            jax.ShapeDtypeStruct(kt_all.shape, F32),
            jax.ShapeDtypeStruct(vt_all.shape, F32),
            jax.ShapeDtypeStruct(lf_all.shape, F32),
            jax.ShapeDtypeStruct((m, d), BF16),
            jax.ShapeDtypeStruct((batch, n_seq_blk, d, MOBA_BLOCK), BF16),
            jax.ShapeDtypeStruct((n_tiles, BF16_ROWS, W_MOBA), F32),
            jax.ShapeDtypeStruct((BF16_ROWS, m), F32),
            jax.ShapeDtypeStruct((H_FOX, m, LANES), F32),
        ]
        out_specs = [
            tok_tile, tok_tile, feat_tile, feat_tile,
            pl.BlockSpec((None, None, H_FOX, tm), lambda i: (layer, i // tps, 0, i % tps)),
            tok_tile,
            pl.BlockSpec((None, nblk, d, MOBA_BLOCK), lambda i: (i // tps, i % tps, 0, 0)),
            pl.BlockSpec((1, BF16_ROWS, W_MOBA), lambda i: (i, 0, 0)),
            pl.BlockSpec((BF16_ROWS, tm), lambda i: (0, i)),
            pl.BlockSpec((H_FOX, tm, LANES), lambda i: (0, i, 0)),
        ]
        scratch = [pltpu.VMEM((8, LANES), F32), pltpu.VMEM((BF16_ROWS, LANES), F32)]
    else:
        out_shape = [jax.ShapeDtypeStruct((m, d), F32)] * 4 + [jax.ShapeDtypeStruct((m, LANES), F32)]
        out_specs = [tok_tile] * 4 + [pl.BlockSpec((tm, LANES), row)]
        scratch = []
    return pl.pallas_call(
        functools.partial(_inproj_kernel, tiles_per_seq=tps, prompt=prompt),
        grid=(n_tiles,),
        in_specs=in_specs,
        out_specs=out_specs,
        out_shape=out_shape,
        scratch_shapes=scratch,
        input_output_aliases=aliases,
        compiler_params=pltpu.CompilerParams(
            dimension_semantics=("arbitrary",), vmem_limit_bytes=VMEM_LIMIT),
        name="inproj_prompt" if prompt else "inproj_sample",
    )(*args)


def _tile_consts():
    krow = _iota((TQ, TQ), 0)
    qcol = _iota((TQ, TQ), 1)
    return krow, qcol, _iota((LANES, TQ), 0)


def _moba_kernel(slopes_ref, q_ref, k_ref, vt_ref, km_ref, o_ref):
    j = pl.program_id(1)
    krow, qcol, frow = _tile_consts()
    qk_off = (qcol - krow).astype(F32)
    causal = krow <= qcol
    brow = _iota((BF16_ROWS, TQ), 0)
    n_blk = km_ref.shape[0]
    r0 = pl.multiple_of(j * TQ, TQ)
    heads = []
    init = []
    for p in range(H_MOBA // 2):
        c0 = p * LANES
        q_t = q_ref[:, c0:c0 + LANES].T
        kmp = jnp.concatenate(
            [km_ref[:, c0:c0 + LANES], jnp.zeros((BF16_ROWS - n_blk, LANES), F32)], axis=0)
        kd = k_ref[pl.ds(r0, TQ), c0:c0 + LANES]
        for hh in range(2):
            f0 = c0 + hh * HEAD_DIM
            in_head = (frow >= HEAD_DIM) if hh else (frow < HEAD_DIM)
            qh = jnp.where(in_head, q_t, 0.0)
            qs = (qh * SCALE).astype(BF16)
            slope = slopes_ref[2 * p + hh]
            g = _nn3(kmp, qh)
            valid = brow < j
            sel = jnp.zeros((BF16_ROWS, TQ), F32)
            for i in range(n_blk - 1):
                gi = g[i:i + 1, :]
                beats = jnp.where(valid & ((g > gi) | ((g == gi) & (brow < i))), 1.0, 0.0)
                cnt = jnp.sum(beats, axis=0, keepdims=True)
                sel = jnp.where(brow == i, jnp.where(cnt < MOBA_TOPK, 1.0, 0.0), sel)
            s = _nn(kd, qs) - slope * qk_off
            s = jnp.where(causal, s, NEG)
            m = jnp.max(s, axis=0, keepdims=True)
            pm = jnp.exp(s - m)
            l = jnp.sum(pm, axis=0, keepdims=True)
            acc = _nn(vt_ref[j, f0:f0 + HEAD_DIM, :], pm.astype(BF16))
            heads.append((f0, qs, slope, sel))
            init.append((m, l, acc))

    def body(i, state):
        ri = pl.multiple_of(i * TQ, TQ)
        dj = ((j - i) * TQ).astype(F32)
        new = []
        for hx, (f0, qs, slope, sel) in enumerate(heads):
            m, l, acc = state[hx]
            c0 = f0 - f0 % LANES
            kt = k_ref[pl.ds(ri, TQ), c0:c0 + LANES]
            taken = jnp.sum(jnp.where(brow == i, sel, 0.0), axis=0, keepdims=True)
            bias = (taken - 1.0) * 1e30 - slope * dj
            s = _nn(kt, qs) - slope * qk_off + bias
            m_new = jnp.maximum(m, jnp.max(s, axis=0, keepdims=True))
            alpha = jnp.exp(m - m_new)
            pm = jnp.exp(s - m_new)
            l = alpha * l + jnp.sum(pm, axis=0, keepdims=True)
            acc = alpha * acc + _nn(vt_ref[i, f0:f0 + HEAD_DIM, :], pm.astype(BF16))
            new.append((m_new, l, acc))
        return tuple(new)

    final = lax.fori_loop(0, j, body, tuple(init))
    for p in range(H_MOBA // 2):
        (_, l0, a0), (_, l1, a1) = final[2 * p], final[2 * p + 1]
        o_ref[:, p * LANES:(p + 1) * LANES] = jnp.concatenate([a0 / l0, a1 / l1], axis=0).T


def _sb_kernel(q_ref, k_ref, vt_ref, o_ref):
    j = pl.program_id(1)
    krow, qcol, frow = _tile_consts()
    strict = krow < qcol
    later = jnp.where(qcol > krow, 1.0, 0.0).astype(BF16)
    r0 = pl.multiple_of(j * TQ, TQ)

    def tile(qs, kt, vt, carry, acc, diag):
        z = _nn(kt, qs)
        lb = _log_sigmoid(z)
        l1 = lb - z
        if diag:
            l1 = jnp.where(strict, l1, 0.0)
        hi, lo = _split2(l1)
        after = _nn(later, hi) + _nn(later, lo) + carry
        a = jnp.exp(lb + after)
        if diag:
            a = jnp.where(strict, a, 0.0)
        acc = acc + _nn(vt, a.astype(BF16))
        carry = carry + jnp.sum(l1, axis=0, keepdims=True)
        return carry, acc

    heads = []
    init = []
    for p in range(H_SB // 2):
        c0 = p * LANES
        q_t = q_ref[:, c0:c0 + LANES].T
        kd = k_ref[pl.ds(r0, TQ), c0:c0 + LANES]
        for hh in range(2):
            f0 = c0 + hh * HEAD_DIM
            in_head = (frow >= HEAD_DIM) if hh else (frow < HEAD_DIM)
            qs = (jnp.where(in_head, q_t, 0.0) * SCALE).astype(BF16)
            heads.append((f0, qs))
            init.append(tile(qs, kd, vt_ref[j, f0:f0 + HEAD_DIM, :], jnp.zeros((1, TQ), F32),
                             jnp.zeros((HEAD_DIM, TQ), F32), True))

    def worst(state):
        top = jnp.max(state[0][0])
        for st in state[1:]:
            top = jnp.maximum(top, jnp.max(st[0]))
        return top

    def cond(st):
        t, top, _ = st
        return jnp.logical_and(t < j, top > SKIP_LOG)

    def body(st):
        t, _, state = st
        i = j - 1 - t
        ri = pl.multiple_of(i * TQ, TQ)
        new = []
        for hx, (f0, qs) in enumerate(heads):
            c0 = f0 - f0 % LANES
            kt = k_ref[pl.ds(ri, TQ), c0:c0 + LANES]
            new.append(tile(qs, kt, vt_ref[i, f0:f0 + HEAD_DIM, :],
                            state[hx][0], state[hx][1], False))
        new = tuple(new)
        return t + 1, worst(new), new

    init = tuple(init)
    _, _, final = lax.while_loop(cond, body, (jnp.int32(0), worst(init), init))
    for p in range(H_SB // 2):
        o_ref[:, p * LANES:(p + 1) * LANES] = jnp.concatenate(
            [final[2 * p][1], final[2 * p + 1][1]], axis=0).T


def _fox_kernel(q_ref, k_ref, vt_ref, cfr_ref, cfrep_ref, o_ref, kmax_sc):
    j = pl.program_id(1)
    krow, qcol, frow = _tile_consts()
    causal = krow <= qcol
    r0 = pl.multiple_of(j * TQ, TQ)

    @pl.when(j == 0)
    def _():
        lane = _iota((1, LANES), 1)
        for p in range(H_FOX // 2):
            col_max = jnp.max(jnp.abs(k_ref[:, p * LANES:(p + 1) * LANES].astype(F32)),
                              axis=0, keepdims=True)
            for hh in range(2):
                in_head = (lane >= HEAD_DIM) if hh else (lane < HEAD_DIM)
                top = jnp.max(jnp.where(in_head, col_max, 0.0), axis=1, keepdims=True)
                hx = 2 * p + hh
                kmax_sc[hx:hx + 1, :] = jnp.broadcast_to(top, (1, LANES))

    heads = []
    init = []
    for p in range(H_FOX // 2):
        c0 = p * LANES
        q_t = q_ref[:, c0:c0 + LANES].T
        kd = k_ref[pl.ds(r0, TQ), c0:c0 + LANES]
        for hh in range(2):
            hx = 2 * p + hh
            f0 = c0 + hh * HEAD_DIM
            in_head = (frow >= HEAD_DIM) if hh else (frow < HEAD_DIM)
            qs = (jnp.where(in_head, q_t, 0.0) * SCALE).astype(BF16)
            fq = cfr_ref[hx, j]
            fk = cfrep_ref[hx, pl.ds(r0, TQ), :]
            s = _nn(kd, qs) + fq - jnp.concatenate([fk, fk], axis=1)
            s = jnp.where(causal, s, NEG)
            m = jnp.max(s, axis=0, keepdims=True)
            pm = jnp.exp(s - m)
            l = jnp.sum(pm, axis=0, keepdims=True)
            acc = _nn(vt_ref[j, f0:f0 + HEAD_DIM, :], pm.astype(BF16))
            cap = jnp.sum(jnp.abs(qs.astype(F32)), axis=0, keepdims=True) * kmax_sc[hx:hx + 1, 0:1] + fq
            heads.append((f0, qs, fq, cap))
            init.append((m, l, acc))

    def headroom(i, state):
        top = None
        for hx, (_, _, _, cap) in enumerate(heads):
            f_end = cfr_ref[hx, jnp.maximum(i, 0)][:, TQ - 1:TQ]
            h = jnp.max(cap - f_end - state[hx][0])
            top = h if top is None else jnp.maximum(top, h)
        return top

    def cond(st):
        t, top, _ = st
        return jnp.logical_and(t < j, top > SKIP_LOG)

    def body(st):
        t, _, state = st
        i = j - 1 - t
        ri = pl.multiple_of(i * TQ, TQ)
        new = []
        for hx, (f0, qs, fq, _) in enumerate(heads):
            m, l, acc = state[hx]
            c0 = f0 - f0 % LANES
            kt = k_ref[pl.ds(ri, TQ), c0:c0 + LANES]
            fk = cfrep_ref[hx, pl.ds(ri, TQ), :]
            s = _nn(kt, qs) + fq - jnp.concatenate([fk, fk], axis=1)
            m_new = jnp.maximum(m, jnp.max(s, axis=0, keepdims=True))
            alpha = jnp.exp(m - m_new)
            pm = jnp.exp(s - m_new)
            l = alpha * l + jnp.sum(pm, axis=0, keepdims=True)
            acc = alpha * acc + _nn(vt_ref[i, f0:f0 + HEAD_DIM, :], pm.astype(BF16))
            new.append((m_new, l, acc))
        new = tuple(new)
        return t + 1, headroom(i - 1, new), new

    init = tuple(init)
    _, _, final = lax.while_loop(cond, body, (jnp.int32(0), headroom(j - 1, init), init))
    for p in range(H_FOX // 2):
        (_, l0, a0), (_, l1, a1) = final[2 * p], final[2 * p + 1]
        o_ref[:, p * LANES:(p + 1) * LANES] = jnp.concatenate([a0 / l0, a1 / l1], axis=0).T


def _prompt_attn(kind, q, ktok, vtb, extra, *, batch, seq_len):
    m = q.shape[0]
    nq = seq_len // TQ
    width, cb = {"moba": (W_MOBA, 0), "sb": (W_SB, W_MOBA // W_SB),
                 "fox": (W_FOX, (W_MOBA + W_SB) // W_FOX)}[kind]
    q_spec = pl.BlockSpec((TQ, width), lambda b, j: (b * nq + j, cb))
    k_spec = pl.BlockSpec((seq_len, width), lambda b, j: (b, cb))
    v_spec = pl.BlockSpec((None, nq, width, TQ), lambda b, j: (b, 0, cb, 0))
    in_specs = [q_spec, k_spec, v_spec]
    args = [q, ktok, vtb]
    scratch = []
    if kind == "moba":
        slopes, km = extra
        in_specs = [pl.BlockSpec(memory_space=pltpu.SMEM)] + in_specs
        in_specs.append(pl.BlockSpec((None, nq, W_MOBA), lambda b, j: (b, 0, 0)))
        args = [slopes] + args + [km]
        body = _moba_kernel
    elif kind == "sb":
        body = _sb_kernel
    else:
        cfr4, cfrep = extra
        in_specs.append(pl.BlockSpec((BF16_ROWS, nq, 1, TQ), lambda b, j: (0, b, 0, 0)))
        in_specs.append(pl.BlockSpec((H_FOX, seq_len, LANES), lambda b, j: (0, b, 0)))
        args += [cfr4, cfrep]
        scratch = [pltpu.VMEM((8, LANES), F32)]
        body = _fox_kernel
    return pl.pallas_call(
        body,
        grid=(batch, nq),
        in_specs=in_specs,
        out_specs=pl.BlockSpec((TQ, width), lambda b, j: (b * nq + j, 0)),
        out_shape=jax.ShapeDtypeStruct((m, width), F32),
        scratch_shapes=scratch,
        compiler_params=pltpu.CompilerParams(
            dimension_semantics=("arbitrary", "arbitrary"), vmem_limit_bytes=VMEM_LIMIT),
        name=kind + "_prompt",
    )(*args)


def _outproj_kernel(oa_ref, ob_ref, oc_ref, g_ref, x_ref, gg_ref, w_ref, gf_ref, y_ref, *, final):
    gg = gg_ref[...]
    a1 = W_MOBA
    a2 = W_MOBA + W_SB
    mix = jnp.concatenate([
        _rms(oa_ref[...]) * gg[:, 0:a1],
        _rms(ob_ref[...]) * gg[:, a1:a2],
        _rms(oc_ref[...]) * gg[:, a2:]], axis=-1)
    gate = g_ref[...]
    mix = mix * (gate * (1.0 / (1.0 + jnp.exp(-gate))))
    y = x_ref[...] + _nn(mix.astype(BF16), w_ref[...])
    if final:
        y = _rms(y) * gf_ref[...]
    y_ref[...] = y


def _outproj(oa, ob, oc, gate, x, gg, w, gf, *, final):
    m = x.shape[0]
    tm = min(TM_PROJ, m)
    d = D_MODEL
    row = lambda i: (i, 0)
    const = lambda i: (0, 0)
    return pl.pallas_call(
        functools.partial(_outproj_kernel, final=final),
        grid=(m // tm,),
        in_specs=[
            pl.BlockSpec((tm, W_MOBA), row),
            pl.BlockSpec((tm, W_SB), row),
            pl.BlockSpec((tm, W_FOX), row),
            pl.BlockSpec((tm, d), row),
            pl.BlockSpec((tm, d), row),
            pl.BlockSpec((1, d), const),
            pl.BlockSpec((d, d), const),
            pl.BlockSpec((1, d), const),
        ],
        out_specs=pl.BlockSpec((tm, d), row),
        out_shape=jax.ShapeDtypeStruct((m, d), F32),
        compiler_params=pltpu.CompilerParams(
            dimension_semantics=("arbitrary",), vmem_limit_bytes=VMEM_LIMIT),
        name="outproj",
    )(oa, ob, oc, gate, x, gg, w, gf)


R_ALL = 128
R_MOBA = 64
R_SB = 32
R_FOX = 32
PAGES_PER_BLOCK = MOBA_BLOCK // PAGE
CHUNK_PAGES = CHUNK_BLOCKS * PAGES_PER_BLOCK


def _sample_attn_kernel(pt_ref, q_ref, kn_ref, vn_ref, lfn_ref, slope_ref, *rest,
                        n_chunks, dec_seq, past_len):
    del pt_ref
    k_refs = rest[0:CHUNK_PAGES]
    v_refs = rest[CHUNK_PAGES:2 * CHUNK_PAGES]
    lf_refs = rest[2 * CHUNK_PAGES:3 * CHUNK_PAGES]
    oa_ref, ob_ref, oc_ref = rest[3 * CHUNK_PAGES:3 * CHUNK_PAGES + 3]
    (qbd_sc, qm_sc, km_sc, mm_sc, lm_sc, accm_sc,
     cs_sc, accs_sc, mf_sc, lf_sc, accf_sc, cf_sc, cn_sc) = rest[3 * CHUNK_PAGES + 3:]
    step = pl.program_id(1)
    nb = n_chunks * CHUNK_BLOCKS
    blk = MOBA_BLOCK
    c_m0, c_s0, c_f0 = 0, W_MOBA, W_MOBA + W_SB
    r_s0, r_f0 = R_MOBA, R_MOBA + R_SB
    lane_m = _iota((R_MOBA, LANES), 1)
    t_m = _iota((R_MOBA, 1), 0) & (dec_seq - 1)
    slope_m = slope_ref[:, 0:1]

    @pl.when(step == 0)
    def _new_rows():
        q = q_ref[...]
        head_of_lane = _iota((dec_seq, D_MODEL), 1) >> 6
        qbd = jnp.concatenate(
            [jnp.where(head_of_lane == h, q, 0.0) for h in range(N_HEADS)], axis=0)
        qbd_b = (qbd * SCALE).astype(BF16)
        qbd_sc[...] = qbd_b
        qm_sc[...] = qbd[0:R_MOBA, 0:W_MOBA]
        km_sc[...] = jnp.zeros_like(km_sc)
        pad = jnp.zeros((LANES - dec_seq, D_MODEL), F32)
        knp = jnp.concatenate([kn_ref[...], pad], axis=0).astype(BF16)
        vnp = jnp.concatenate([vn_ref[...], pad], axis=0).astype(BF16)
        s_all = _nt(qbd_b, knp)
        c = _iota((1, LANES), 1)

        s = s_all[0:R_MOBA] - slope_m * (t_m - c).astype(F32)
        s = jnp.where(c <= t_m, s, NEG)
        m = jnp.max(s, axis=-1, keepdims=True)
        pm = jnp.exp(s - m)
        l = jnp.sum(pm, axis=-1, keepdims=True)
        mm_sc[...] = jnp.where(lane_m == nb, m, NEG)
        lm_sc[...] = jnp.where(lane_m == nb, l, 0.0)
        accm_sc[nb] = _nn(pm.astype(BF16), vnp[:, c_m0:c_m0 + W_MOBA])

        t_s = _iota((R_SB, 1), 0) & (dec_seq - 1)
        z = s_all[r_s0:r_s0 + R_SB]
        lb = _log_sigmoid(z)
        mask_s = c < t_s
        l1 = jnp.where(mask_s, lb - z, 0.0)
        rr = _iota((LANES, LANES), 0)
        cc = _iota((LANES, LANES), 1)
        after_m = jnp.where(rr > cc, 1.0, 0.0).astype(BF16)
        hi, lo = _split2(l1)
        after = _nn(hi, after_m) + _nn(lo, after_m)
        a = jnp.where(mask_s, jnp.exp(lb + after), 0.0)
        accs_sc[...] = _nn(a.astype(BF16), vnp[:, c_s0:c_s0 + W_SB])
        cs_sc[...] = jnp.broadcast_to(jnp.sum(l1, axis=-1, keepdims=True), cs_sc.shape)

        t_f = t_s
        lfpad = jnp.concatenate(
            [lfn_ref[...], jnp.zeros((LANES - dec_seq, LANES), F32)], axis=0)
        head_of_row = _iota((R_FOX, LANES), 0) >> 3
        expand = jnp.where(_iota((R_FOX, LANES), 1) == head_of_row, 1.0, 0.0).astype(BF16)
        lfrow = _nt_exact_lhs(expand, lfpad)
        incl_m = jnp.where(rr <= cc, 1.0, 0.0).astype(BF16)
        cnrow = _nn_exact_rhs(lfrow, incl_m)
        cncol = jnp.sum(jnp.where(c == t_f, cnrow, 0.0), axis=-1, keepdims=True)
        s = s_all[r_f0:r_f0 + R_FOX] + cncol - cnrow
        s = jnp.where(c <= t_f, s, NEG)
        m = jnp.max(s, axis=-1, keepdims=True)
        pm = jnp.exp(s - m)
        mf_sc[...] = jnp.broadcast_to(m, mf_sc.shape)
        lf_sc[...] = jnp.broadcast_to(jnp.sum(pm, axis=-1, keepdims=True), lf_sc.shape)
        accf_sc[...] = _nn(pm.astype(BF16), vnp[:, c_f0:c_f0 + W_FOX])
        cn_sc[...] = jnp.broadcast_to(cncol, cn_sc.shape)
        cf_sc[...] = jnp.zeros_like(cf_sc)

    @pl.when(step > 0)
    def _past_chunk():
        chunk = n_chunks - step
        qbd = qbd_sc[...]
        c = _iota((1, blk), 1)
        rr = _iota((blk, blk), 0)
        cc = _iota((blk, blk), 1)
        after_m = jnp.where(rr > cc, 1.0, 0.0).astype(BF16)
        lane_km = _iota(km_sc.shape, 1)

        def block_rows(refs, b, r0, width):
            return jnp.concatenate(
                [refs[PAGES_PER_BLOCK * b + o][r0:r0 + width, :] for o in range(PAGES_PER_BLOCK)],
                axis=1)

        km = km_sc[...]
        mm = mm_sc[...]
        lm = lm_sc[...]
        q_m = qbd[0:R_MOBA, c_m0:c_m0 + W_MOBA]
        for b in range(CHUNK_BLOCKS):
            pb = chunk * CHUNK_BLOCKS + b
            kf = block_rows(k_refs, b, c_m0, W_MOBA)
            km = jnp.where(lane_km == pb, jnp.mean(kf, axis=1, keepdims=True), km)
            s = _nn(q_m, kf.astype(BF16))
            dist = (past_len + t_m - (pb * blk + c)).astype(F32)
            s = s - slope_m * dist
            m = jnp.max(s, axis=-1, keepdims=True)
            pm = jnp.exp(s - m)
            l = jnp.sum(pm, axis=-1, keepdims=True)
            mm = jnp.where(lane_m == pb, m, mm)
            lm = jnp.where(lane_m == pb, l, lm)
            accm_sc[pb] = _nt(pm.astype(BF16), block_rows(v_refs, b, c_m0, W_MOBA).astype(BF16))
        km_sc[...] = km
        mm_sc[...] = mm
        lm_sc[...] = lm

        q_s = qbd[r_s0:r_s0 + R_SB, c_s0:c_s0 + W_SB]
        carry = cs_sc[:, 0:1]
        acc = accs_sc[...]
        for b in reversed(range(CHUNK_BLOCKS)):
            z = _nn(q_s, block_rows(k_refs, b, c_s0, W_SB).astype(BF16))
            lb = _log_sigmoid(z)
            l1 = lb - z
            hi, lo = _split2(l1)
            after = _nn(hi, after_m) + _nn(lo, after_m) + carry
            a = jnp.exp(lb + after)
            acc = acc + _nt(a.astype(BF16), block_rows(v_refs, b, c_s0, W_SB).astype(BF16))
            carry = carry + jnp.sum(l1, axis=-1, keepdims=True)
        accs_sc[...] = acc
        cs_sc[...] = jnp.broadcast_to(carry, cs_sc.shape)

        q_f = qbd[r_f0:r_f0 + R_FOX, c_f0:c_f0 + W_FOX]
        cn = cn_sc[:, 0:1]
        cf = cf_sc[:, 0:1]
        logits = []
        for b in reversed(range(CHUNK_BLOCKS)):
            lf = jnp.concatenate(
                [lf_refs[PAGES_PER_BLOCK * b + o][...] for o in range(PAGES_PER_BLOCK)], axis=1)
            lfe = jnp.concatenate(
                [jnp.broadcast_to(lf[h:h + 1, :], (dec_seq, blk)) for h in range(H_FOX)], axis=0)
            suffix = _nn_exact_rhs(lfe, after_m) + cf
            s = _nn(q_f, block_rows(k_refs, b, c_f0, W_FOX).astype(BF16)) + cn + suffix
            logits.append((b, s))
            cf = cf + jnp.sum(lfe, axis=-1, keepdims=True)
        m_old = mf_sc[:, 0:1]
        m_new = m_old
        for _, s in logits:
            m_new = jnp.maximum(m_new, jnp.max(s, axis=-1, keepdims=True))
        alpha = jnp.exp(m_old - m_new)
        l_new = alpha * lf_sc[:, 0:1]
        acc = alpha * accf_sc[...]
        for b, s in logits:
            pm = jnp.exp(s - m_new)
            l_new = l_new + jnp.sum(pm, axis=-1, keepdims=True)
            acc = acc + _nt(pm.astype(BF16), block_rows(v_refs, b, c_f0, W_FOX).astype(BF16))
        accf_sc[...] = acc
        mf_sc[...] = jnp.broadcast_to(m_new, mf_sc.shape)
        lf_sc[...] = jnp.broadcast_to(l_new, lf_sc.shape)
        cf_sc[...] = jnp.broadcast_to(cf, cf_sc.shape)

    @pl.when(step == n_chunks)
    def _finish():
        g = _nn3(qm_sc[...], km_sc[...])
        lane_f = lane_m.astype(F32)
        left = jnp.where(lane_m < nb, g, NEG)
        selm = jnp.where(lane_m == nb, 1.0, 0.0)
        for _ in range(min(MOBA_TOPK, nb)):
            top = jnp.max(left, axis=-1, keepdims=True)
            first = jnp.min(jnp.where(left == top, lane_f, float(LANES)), axis=-1, keepdims=True)
            pick = lane_f == first
            selm = jnp.where(pick, 1.0, selm)
            left = jnp.where(pick, NEG, left)
        taken = selm > 0.5
        mall = mm_sc[...]
        mfin = jnp.max(jnp.where(taken, mall, NEG), axis=-1, keepdims=True)
        w = jnp.where(taken, jnp.exp(jnp.minimum(mall - mfin, 0.0)), 0.0)
        lfin = jnp.sum(w * lm_sc[...], axis=-1, keepdims=True)
        acc = jnp.zeros((R_MOBA, W_MOBA), F32)
        for b in range(nb + 1):
            acc = acc + w[:, b:b + 1] * accm_sc[b]
        om = acc / lfin

        def diag_blocks(o, n_heads):
            width = n_heads * HEAD_DIM
            head_of_lane = _iota((dec_seq, width), 1) >> 6
            out = jnp.zeros((dec_seq, width), F32)
            for h in range(n_heads):
                out = out + jnp.where(head_of_lane == h, o[h * dec_seq:(h + 1) * dec_seq, :], 0.0)
            return out

        oa_ref[...] = diag_blocks(om, H_MOBA)
        ob_ref[...] = diag_blocks(accs_sc[...], H_SB)
        oc_ref[...] = diag_blocks(accf_sc[...] / lf_sc[:, 0:1], H_FOX)


def _sample_attn(pt, q, kn, vn, lfn, slope_rows, ckt, cvt, clf, *, layer, dec_batch, dec_seq, n_pages):
    m = q.shape[0]
    d = D_MODEL
    n_chunks = n_pages // CHUNK_PAGES
    nb = n_chunks * CHUNK_BLOCKS
    past_len = n_pages * PAGE

    def page_map(off):
        def index_map(b, s, pt_ref):
            chunk = n_chunks - jnp.maximum(s, 1)
            return (layer, pt_ref[b * n_pages + CHUNK_PAGES * chunk + off], 0, 0)
        return index_map

    tok = lambda b, s, pt_ref: (b, 0)
    const = lambda b, s, pt_ref: (0, 0)
    page_kv = [pl.BlockSpec((None, None, d, PAGE), page_map(o)) for o in range(CHUNK_PAGES)]
    page_lf = [pl.BlockSpec((None, None, H_FOX, PAGE), page_map(o)) for o in range(CHUNK_PAGES)]
    grid_spec = pltpu.PrefetchScalarGridSpec(
        num_scalar_prefetch=1,
        grid=(dec_batch, n_chunks + 1),
        in_specs=[
            pl.BlockSpec((dec_seq, d), tok),
            pl.BlockSpec((dec_seq, d), tok),
            pl.BlockSpec((dec_seq, d), tok),
            pl.BlockSpec((dec_seq, LANES), tok),
            pl.BlockSpec((R_MOBA, LANES), const),
        ] + page_kv + page_kv + page_lf,
        out_specs=[
            pl.BlockSpec((dec_seq, W_MOBA), tok),
            pl.BlockSpec((dec_seq, W_SB), tok),
            pl.BlockSpec((dec_seq, W_FOX), tok),
        ],
        scratch_shapes=[
            pltpu.VMEM((R_ALL, d), BF16),
            pltpu.VMEM((R_MOBA, W_MOBA), F32),
            pltpu.VMEM((W_MOBA, LANES), F32),
            pltpu.VMEM((R_MOBA, LANES), F32),
            pltpu.VMEM((R_MOBA, LANES), F32),
            pltpu.VMEM((nb + 1, R_MOBA, W_MOBA), F32),
            pltpu.VMEM((R_SB, LANES), F32),
            pltpu.VMEM((R_SB, W_SB), F32),
            pltpu.VMEM((R_FOX, LANES), F32),
            pltpu.VMEM((R_FOX, LANES), F32),
            pltpu.VMEM((R_FOX, W_FOX), F32),
            pltpu.VMEM((R_FOX, LANES), F32),
            pltpu.VMEM((R_FOX, LANES), F32),
        ],
    )
    return pl.pallas_call(
        functools.partial(_sample_attn_kernel, n_chunks=n_chunks, dec_seq=dec_seq, past_len=past_len),
        grid_spec=grid_spec,
        out_shape=[jax.ShapeDtypeStruct((m, W_MOBA), F32),
                   jax.ShapeDtypeStruct((m, W_SB), F32),
                   jax.ShapeDtypeStruct((m, W_FOX), F32)],
        compiler_params=pltpu.CompilerParams(
            dimension_semantics=("arbitrary", "arbitrary"), vmem_limit_bytes=VMEM_LIMIT),
        name="sample_attn",
    )(pt, q, kn, vn, lfn, slope_rows, *([ckt] * CHUNK_PAGES), *([cvt] * CHUNK_PAGES),
      *([clf] * CHUNK_PAGES))


def kernel(x_prompt, x_sample, cache_k, cache_v, cache_logf, page_table,
           norm_g, w_in, b_f, g_grp, w_out, g_final):
    batch, seq_len, d = x_prompt.shape
    dec_batch, dec_seq, _ = x_sample.shape
    depth, n_phys = cache_k.shape[0], cache_k.shape[1]
    n_pages = page_table.shape[1]
    assert d == D_MODEL and seq_len % TM_PROJ == 0 and cache_k.shape[2] == PAGE
    assert seq_len // MOBA_BLOCK <= BF16_ROWS
    assert dec_seq == 8 and n_pages % CHUNK_PAGES == 0 and n_pages // PAGES_PER_BLOCK < LANES

    hp = x_prompt.reshape(batch * seq_len, d)
    hs = x_sample.reshape(dec_batch * dec_seq, d)
    ckt = jnp.transpose(cache_k, (0, 1, 3, 4, 2)).reshape(depth, n_phys, d, PAGE)
    cvt = jnp.transpose(cache_v, (0, 1, 3, 4, 2)).reshape(depth, n_phys, d, PAGE)
    clf = jnp.swapaxes(cache_logf, 2, 3)
    pt = page_table.reshape(-1).astype(jnp.int32)
    slopes = jnp.asarray([2.0 ** (-8.0 * (i + 1) / H_MOBA) for i in range(H_MOBA)], dtype=F32)
    slope_rows = jnp.broadcast_to(jnp.repeat(slopes, dec_seq)[:, None], (R_MOBA, LANES))
    gf = g_final.reshape(1, d)
    w_in_t = jnp.transpose(w_in, (2, 0, 1))

    bufs = (jnp.zeros((depth, batch, d, seq_len), F32),
            jnp.zeros((depth, batch, d, seq_len), F32),
            jnp.zeros((depth, batch, H_FOX, seq_len), F32))
    ksm, vsm, fsm = [], [], []
    nblk = seq_len // MOBA_BLOCK
    for l in range(depth):
        wt = w_in_t[:4 * d, l, :].astype(BF16)
        wft = jnp.pad(w_in_t[4 * d:, l, :], ((0, LANES - H_FOX), (0, 0)))
        bfc = jnp.pad(b_f[l].reshape(1, H_FOX), ((0, 0), (0, LANES - H_FOX)))
        bfr = jnp.broadcast_to(jnp.pad(b_f[l], (0, BF16_ROWS - H_FOX))[:, None], (BF16_ROWS, LANES))
        ng = norm_g[l].reshape(1, d)
        gg = g_grp[l].reshape(1, d)
        wo = w_out[l].astype(BF16)
        final = l == depth - 1

        q, gate, kt_all, vt_all, lf_all, ktok, vtb, km, cfr, cfrep = _inproj(
            hp, ng, wt, wft, bfc, bfr, bufs, layer=l, batch=batch, seq_len=seq_len, prompt=True)
        bufs = (kt_all, vt_all, lf_all)
        km = km[:, :TM_PROJ // MOBA_BLOCK, :].reshape(batch, nblk, W_MOBA)
        cfr4 = cfr.reshape(BF16_ROWS, batch * nblk, 1, TQ)
        oa = _prompt_attn("moba", q, ktok, vtb, (slopes, km), batch=batch, seq_len=seq_len)
        ob = _prompt_attn("sb", q, ktok, vtb, None, batch=batch, seq_len=seq_len)
        oc = _prompt_attn("fox", q, ktok, vtb, (cfr4, cfrep), batch=batch, seq_len=seq_len)
        hp = _outproj(oa, ob, oc, gate, hp, gg, wo, gf, final=final)

        q, k, v, gate, lf = _inproj(
            hs, ng, wt, wft, bfc, bfr, batch=dec_batch, seq_len=dec_seq, prompt=False)
        oa, ob, oc = _sample_attn(pt, q, k, v, lf, slope_rows, ckt, cvt, clf, layer=l,
                                  dec_batch=dec_batch, dec_seq=dec_seq, n_pages=n_pages)
        hs = _outproj(oa, ob, oc, gate, hs, gg, wo, gf, final=final)
        ksm.append(k)
        vsm.append(v)
        fsm.append(lf[:, :H_FOX])

    hd = (N_HEADS, HEAD_DIM)

    def token_major(x):
        return jnp.transpose(x.reshape((depth, batch) + hd + (seq_len,)), (0, 1, 4, 2, 3))

    kt_all, vt_all, lf_all = bufs
    return (hp.reshape(batch, seq_len, d),
            hs.reshape(dec_batch, dec_seq, d),
            token_major(kt_all),
            token_major(vt_all),
            jnp.swapaxes(lf_all, 2, 3),
            jnp.stack(ksm).reshape((depth, dec_batch, dec_seq) + hd),
            jnp.stack(vsm).reshape((depth, dec_batch, dec_seq) + hd),
            jnp.stack(fsm).reshape(depth, dec_batch, dec_seq, H_FOX))
```
